```python
import jax, jax.numpy as jnp
from jax import lax
import numpy as np

D_MODEL = 2048
BATCH = 2
SEQ = 4096
DEPTH = 1

MEM_LEN = 256
FOX_HEADS = 8
FOX_HEAD_DIM = 128
FOX_WIDTH = FOX_HEADS * FOX_HEAD_DIM
MLSTM_HEADS = 4
MLSTM_HEAD_DIM = 256
MLSTM_WIDTH = MLSTM_HEADS * MLSTM_HEAD_DIM
MEM_HEADS = 4
MEM_HEAD_DIM = 256
MEM_WIDTH = MEM_HEADS * MEM_HEAD_DIM
N_BRANCH = 3
CONV_WIDTH = 4
Q_BLOCK = 128
CHUNK = 128
D_FF = -(-8 * D_MODEL // (3 * 256)) * 256
EPS = 1e-6
FOX_FORGET_BIAS_LO = 1.0
FOX_FORGET_BIAS_HI = 5.0
MLSTM_FORGET_BIAS_LO = 3.0
MLSTM_FORGET_BIAS_HI = 6.0

IN_SIZES = (FOX_WIDTH, FOX_WIDTH, FOX_WIDTH, FOX_HEADS,
            MLSTM_WIDTH, MLSTM_WIDTH, MLSTM_WIDTH, MLSTM_HEADS, MLSTM_HEADS, MLSTM_WIDTH,
            MEM_WIDTH, N_BRANCH * D_MODEL)
IN_SPLITS = tuple(sum(IN_SIZES[:i + 1]) for i in range(len(IN_SIZES) - 1))
PROJ_WIDTH = sum(IN_SIZES)
FOX_F_OFF = sum(IN_SIZES[:3])
MLSTM_F_OFF = sum(IN_SIZES[:8])

kernel_name = "hybrid_fox_mlstm_memxattn_gated"


def rmsnorm(x, g):
    xf = x.astype(jnp.float32)
    y = xf * lax.rsqrt(jnp.mean(xf * xf, axis=-1, keepdims=True) + EPS)
    return (y * g.astype(jnp.float32)).astype(x.dtype)


def split_heads(t, n):
    b, s, w = t.shape
    return t.reshape(b, s, n, w // n).transpose(0, 2, 1, 3)


def merge_heads(t):
    b, h, s, d = t.shape
    return t.transpose(0, 2, 1, 3).reshape(b, s, h * d)


def causal_depthwise_conv(x, w, bias):
    k = w.shape[0]
    y = lax.conv_general_dilated(x, w[:, None, :], window_strides=(1,), padding=[(k - 1, 0)],
                                 dimension_numbers=('NWC', 'WIO', 'NWC'),
                                 feature_group_count=x.shape[-1])
    return y + bias


def forgetting_attention(q, k, v, log_f):
    b, h, s, dh = q.shape
    nb = s // Q_BLOCK
    F = jnp.cumsum(log_f, axis=-1)
    qb = q.reshape(b, h, nb, Q_BLOCK, dh).transpose(2, 0, 1, 3, 4)
    Fb = F.reshape(b, h, nb, Q_BLOCK).transpose(2, 0, 1, 3)
    kpos = jnp.arange(s)
    scale = dh ** -0.5

    def block(args):
        qi, Fi, i = args
        qpos = i * Q_BLOCK + jnp.arange(Q_BLOCK)
        logits = (jnp.einsum('bhqd,bhkd->bhqk', qi, k) * scale
                  + Fi[..., :, None] - F[..., None, :])
        logits = jnp.where(kpos[None, :] <= qpos[:, None], logits, -jnp.inf)
        p = jax.nn.softmax(logits, axis=-1)
        return jnp.einsum('bhqk,bhkd->bhqd', p, v)

    out = lax.map(block, (qb, Fb, jnp.arange(nb)))
    return out.transpose(1, 2, 0, 3, 4).reshape(b, h, s, dh)


def mlstm_chunkwise(q, k, v, i_pre, log_f):
    b, h, s, dk = q.shape
    dv = v.shape[-1]
    nc = s // CHUNK

    def chunks(a):
        return jnp.moveaxis(a.reshape(b, h, nc, CHUNK, *a.shape[3:]), 2, 0)

    causal = jnp.tril(jnp.ones((CHUNK, CHUNK), dtype=bool))

    def step(carry, xs):
        C, n, m = carry
        qj, kj, vj, ij, fj = xs
        bcum = jnp.cumsum(fj, axis=-1)
        dlog = bcum[..., :, None] - bcum[..., None, :] + ij[..., None, :]
        dlog = jnp.where(causal, dlog, -jnp.inf)
        inter = bcum + m[..., None]
        m_t = jnp.maximum(inter, jnp.max(dlog, axis=-1))
        w = jnp.exp(dlog - m_t[..., None])
        a = jnp.exp(inter - m_t)
        qk = jnp.einsum('bhtd,bhsd->bhts', qj, kj) * w
        num = (a[..., None] * jnp.einsum('bhtd,bhde->bhte', qj, C)
               + jnp.einsum('bhts,bhse->bhte', qk, vj))
        den = a * jnp.einsum('bhtd,bhd->bht', qj, n) + jnp.sum(qk, axis=-1)
        h_out = num / jnp.maximum(jnp.abs(den), jnp.exp(-m_t))[..., None]
        b_end = bcum[..., -1]
        g = b_end[..., None] - bcum + ij
        m_new = jnp.maximum(b_end + m, jnp.max(g, axis=-1))
        decay = jnp.exp(b_end + m - m_new)
        ws = jnp.exp(g - m_new[..., None])
        C_new = decay[..., None, None] * C + jnp.einsum('bhs,bhsd,bhse->bhde', ws, kj, vj)
        n_new = decay[..., None] * n + jnp.einsum('bhs,bhsd->bhd', ws, kj)
        return (C_new, n_new, m_new), h_out

    init = (jnp.zeros((b, h, dk, dv), jnp.float32), jnp.zeros((b, h, dk), jnp.float32),
            jnp.zeros((b, h), jnp.float32))
    _, hs = lax.scan(step, init, (chunks(q), chunks(k), chunks(v), chunks(i_pre), chunks(log_f)))
    return jnp.moveaxis(hs, 0, 2).reshape(b, h, s, dv)


def head_rmsnorm(t, g):
    y = t * lax.rsqrt(jnp.mean(t * t, axis=-1, keepdims=True) + EPS)
    return merge_heads(y) * g.astype(jnp.float32)


def memory_attention(q, mk, mv):
    logits = jnp.einsum('bhqd,bhmd->bhqm', q, mk) * (q.shape[-1] ** -0.5)
    p = jax.nn.softmax(logits, axis=-1)
    return jnp.einsum('bhqm,bhmd->bhqd', p, mv)


def setup_inputs(seed: int = 0) -> dict:
    key = jax.random.key(seed)
    ks = jax.random.split(key, 20)
    nrm = lambda k, shape: jax.random.normal(k, shape, jnp.float32)
    x = nrm(ks[0], (BATCH, SEQ, D_MODEL))
    mem = nrm(ks[1], (BATCH, MEM_LEN, D_MODEL))
    norm_mix = 1.0 + 0.02 * nrm(ks[2], (DEPTH, D_MODEL))
    w_in = nrm(ks[3], (DEPTH, D_MODEL, PROJ_WIDTH)) * D_MODEL ** -0.5
    b_in = 0.01 * nrm(ks[4], (DEPTH, PROJ_WIDTH))
    b_in = b_in.at[:, FOX_F_OFF:FOX_F_OFF + FOX_HEADS].add(
        jnp.linspace(FOX_FORGET_BIAS_LO, FOX_FORGET_BIAS_HI, FOX_HEADS))
    b_in = b_in.at[:, MLSTM_F_OFF:MLSTM_F_OFF + MLSTM_HEADS].add(
        jnp.linspace(MLSTM_FORGET_BIAS_LO, MLSTM_FORGET_BIAS_HI, MLSTM_HEADS))
    conv_w = nrm(ks[5], (DEPTH, CONV_WIDTH, 2 * MLSTM_WIDTH)) * CONV_WIDTH ** -0.5
    conv_b = 0.01 * nrm(ks[6], (DEPTH, 2 * MLSTM_WIDTH))
    mlstm_norm = 1.0 + 0.02 * nrm(ks[7], (DEPTH, MLSTM_WIDTH))
    norm_mem = 1.0 + 0.02 * nrm(ks[8], (DEPTH, D_MODEL))
    w_mem_kv = nrm(ks[9], (DEPTH, D_MODEL, 2 * MEM_WIDTH)) * D_MODEL ** -0.5
    w_br_fox = nrm(ks[10], (DEPTH, FOX_WIDTH, D_MODEL)) * FOX_WIDTH ** -0.5
    w_br_mlstm = nrm(ks[11], (DEPTH, MLSTM_WIDTH, D_MODEL)) * MLSTM_WIDTH ** -0.5
    w_br_mem = nrm(ks[12], (DEPTH, MEM_WIDTH, D_MODEL)) * MEM_WIDTH ** -0.5
    w_out = nrm(ks[13], (DEPTH, D_MODEL, D_MODEL)) * D_MODEL ** -0.5
    norm_ffn = 1.0 + 0.02 * nrm(ks[14], (DEPTH, D_MODEL))
    w_ffn_in = nrm(ks[15], (DEPTH, D_MODEL, 2 * D_FF)) * D_MODEL ** -0.5
    w_ffn_out = nrm(ks[16], (DEPTH, D_FF, D_MODEL)) * D_FF ** -0.5
    norm_final = 1.0 + 0.02 * nrm(ks[17], (D_MODEL,))
    return {"x": x, "mem": mem, "norm_mix": norm_mix, "w_in": w_in, "b_in": b_in,
            "conv_w": conv_w, "conv_b": conv_b, "mlstm_norm": mlstm_norm,
            "norm_mem": norm_mem, "w_mem_kv": w_mem_kv, "w_br_fox": w_br_fox,
            "w_br_mlstm": w_br_mlstm, "w_br_mem": w_br_mem, "w_out": w_out,
            "norm_ffn": norm_ffn, "w_ffn_in": w_ffn_in, "w_ffn_out": w_ffn_out,
            "norm_final": norm_final}


def reference(x, mem, norm_mix, w_in, b_in, conv_w, conv_b, mlstm_norm, norm_mem, w_mem_kv,
              w_br_fox, w_br_mlstm, w_br_mem, w_out, norm_ffn, w_ffn_in, w_ffn_out, norm_final):
    f32 = jnp.float32
    h = x
    bsz, seq, _ = x.shape
    for l in range(DEPTH):
        u = rmsnorm(h, norm_mix[l])
        proj = u @ w_in[l] + b_in[l]
        (fq, fk, fv, ff, lq, lk, lv, li, lf, lo, mq, gates) = jnp.split(proj, IN_SPLITS, axis=-1)

        fox_logf = jax.nn.log_sigmoid(ff.astype(f32)).transpose(0, 2, 1)
        y_fox = forgetting_attention(split_heads(fq, FOX_HEADS).astype(f32),
                                     split_heads(fk, FOX_HEADS).astype(f32),
                                     split_heads(fv, FOX_HEADS).astype(f32), fox_logf)
        y_fox = merge_heads(y_fox).astype(x.dtype)

        qk_in = jax.nn.silu(causal_depthwise_conv(jnp.concatenate([lq, lk], axis=-1),
                                                  conv_w[l], conv_b[l]))
        lq_c, lk_c = jnp.split(qk_in, 2, axis=-1)
        h_ml = mlstm_chunkwise(split_heads(lq_c, MLSTM_HEADS).astype(f32),
                               split_heads(lk_c, MLSTM_HEADS).astype(f32) * (MLSTM_HEAD_DIM ** -0.5),
                               split_heads(lv, MLSTM_HEADS).astype(f32),
                               li.astype(f32).transpose(0, 2, 1),
                               jax.nn.log_sigmoid(lf.astype(f32)).transpose(0, 2, 1))
        y_ml = (head_rmsnorm(h_ml, mlstm_norm[l]) * jax.nn.sigmoid(lo.astype(f32))).astype(x.dtype)

        mkv = rmsnorm(mem, norm_mem[l]) @ w_mem_kv[l]
        mk, mv = jnp.split(mkv, 2, axis=-1)
        y_mem = memory_attention(split_heads(mq, MEM_HEADS).astype(f32),
                                 split_heads(mk, MEM_HEADS).astype(f32),
                                 split_heads(mv, MEM_HEADS).astype(f32))
        y_mem = merge_heads(y_mem).astype(x.dtype)

        g = jax.nn.sigmoid(gates.astype(f32)).astype(x.dtype).reshape(bsz, seq, N_BRANCH, D_MODEL)
        merged = (g[:, :, 0] * (y_fox @ w_br_fox[l])
                  + g[:, :, 1] * (y_ml @ w_br_mlstm[l])
                  + g[:, :, 2] * (y_mem @ w_br_mem[l]))
        h = h + merged @ w_out[l]

        u = rmsnorm(h, norm_ffn[l])
        gate_ff, up_ff = jnp.split(u @ w_ffn_in[l], 2, axis=-1)
        h = h + (jax.nn.silu(gate_ff) * up_ff) @ w_ffn_out[l]
    return rmsnorm(h, norm_final)
```

```python
import functools

import jax
import jax.numpy as jnp
from jax import lax
from jax.experimental import pallas as pl
from jax.experimental.pallas import tpu as pltpu

D_MODEL = 2048
FOX_HEADS = 8
FOX_HEAD_DIM = 128
FOX_WIDTH = FOX_HEADS * FOX_HEAD_DIM
MLSTM_HEADS = 4
MLSTM_HEAD_DIM = 256
MLSTM_WIDTH = MLSTM_HEADS * MLSTM_HEAD_DIM
MEM_HEADS = 4
MEM_HEAD_DIM = 256
MEM_WIDTH = MEM_HEADS * MEM_HEAD_DIM
N_BRANCH = 3
CONV_WIDTH = 4
CHUNK = 128
HALO = 16
D_FF = 5632
EPS = 1e-6

OFF_FF = 3 * FOX_WIDTH
OFF_LQ = OFF_FF + FOX_HEADS
OFF_LI = OFF_LQ + 3 * MLSTM_WIDTH
OFF_LO = OFF_LI + 2 * MLSTM_HEADS
N_SMALL = FOX_HEADS + 2 * MLSTM_HEADS
PROJ_BIG = 3 * FOX_WIDTH + 3 * MLSTM_WIDTH + MLSTM_WIDTH + MEM_WIDTH + N_BRANCH * D_MODEL

COL_LQ = 3 * FOX_WIDTH
COL_LK = COL_LQ + MLSTM_WIDTH
COL_LV = COL_LK + MLSTM_WIDTH
COL_LO = COL_LV + MLSTM_WIDTH
COL_MQ = COL_LO + MLSTM_WIDTH
COL_GATES = COL_MQ + MEM_WIDTH

ROW_FOX = 0
ROW_MLB = FOX_HEADS
ROW_MLI = FOX_HEADS + MLSTM_HEADS

V7X_LANES = 128
V7X_VMEM_LIMIT = 56 * 1024 * 1024

F32 = jnp.float32
BF16 = jnp.bfloat16


def _params(*sem):
    return pltpu.CompilerParams(dimension_semantics=sem, vmem_limit_bytes=V7X_VMEM_LIMIT)


def _rms(x, g):
    return x * lax.rsqrt(jnp.mean(x * x, axis=-1, keepdims=True) + EPS) * g


def _dot(a, b):
    return jnp.dot(a, b, preferred_element_type=F32)


def _dot_nt(a, b):
    return lax.dot_general(a, b, (((1,), (1,)), ((), ())), preferred_element_type=F32)


def _dot_tn(a, b):
    return lax.dot_general(a, b, (((0,), (0,)), ((), ())), preferred_element_type=F32)


def _log_sigmoid(x):
    return jnp.minimum(x, 0.0) - jnp.log1p(jnp.exp(-jnp.abs(x)))


def _in_proj_kernel(x_ref, g_ref, w_ref, b_ref, wst_ref, bst_ref, o_ref, gst_ref, u_ref):
    @pl.when(pl.program_id(1) == 0)
    def _():
        u = _rms(x_ref[...], g_ref[...]).astype(BF16)
        u_ref[...] = u
        gst_ref[...] = _dot_nt(wst_ref[...], u) + bst_ref[...]

    o_ref[...] = (_dot(u_ref[...], w_ref[...]) + b_ref[...]).astype(BF16)


def _in_proj(x2, g, w_big, b_big, w_small_t, b_small_t, bm=512, bn=1024):
    m = x2.shape[0]
    return pl.pallas_call(
        _in_proj_kernel,
        grid=(m // bm, PROJ_BIG // bn),
        in_specs=[
            pl.BlockSpec((bm, D_MODEL), lambda i, j: (i, 0)),
            pl.BlockSpec((1, D_MODEL), lambda i, j: (0, 0)),
            pl.BlockSpec((D_MODEL, bn), lambda i, j: (0, j)),
            pl.BlockSpec((1, bn), lambda i, j: (0, j)),
            pl.BlockSpec((N_SMALL, D_MODEL), lambda i, j: (0, 0)),
            pl.BlockSpec((N_SMALL, 1), lambda i, j: (0, 0)),
        ],
        out_specs=[
            pl.BlockSpec((bm, bn), lambda i, j: (i, j)),
            pl.BlockSpec((N_SMALL, bm), lambda i, j: (0, i)),
        ],
        out_shape=[
            jax.ShapeDtypeStruct((m, PROJ_BIG), BF16),
            jax.ShapeDtypeStruct((N_SMALL, m), F32),
        ],
        scratch_shapes=[pltpu.VMEM((bm, D_MODEL), BF16)],
        compiler_params=_params("parallel", "arbitrary"),
        name="in_proj",
    )(x2, g, w_big, b_big, w_small_t, b_small_t)


def _gates_kernel(gst_ref, rows_ref, krows_ref, cols_ref, *, seq, tk):
    g = gst_ref[...]
    row = lax.broadcasted_iota(jnp.int32, g.shape, 0)
    lane = lax.broadcasted_iota(jnp.int32, g.shape, 1)
    in_chunk = lane % CHUNK
    is_input_gate = row >= ROW_MLI
    local = jnp.where(is_input_gate, g, _log_sigmoid(g))
    d = 1
    while d < CHUNK:
        local = local + jnp.where(in_chunk >= d, pltpu.roll(local, d, axis=1), 0.0)
        d *= 2
    carry = pltpu.roll(jnp.where(in_chunk == CHUNK - 1, local, 0.0), 1, axis=1)
    carry = jnp.where(lane == 0, 0.0, carry)
    d = 1
    while d < CHUNK:
        carry = carry + jnp.where(in_chunk >= d, pltpu.roll(carry, d, axis=1), 0.0)
        d *= 2
    while d < seq:
        carry = carry + jnp.concatenate([jnp.zeros((N_SMALL, d), F32), carry[:, :seq - d]], axis=1)
        d *= 2
    out = jnp.where(row < ROW_MLB, local + carry, jnp.where(is_input_gate, g, local))
    rows_ref[0] = out
    for c in range(seq // tk):
        krows_ref[0, c] = out[:, c * tk:(c + 1) * tk]
    padded = jnp.concatenate([out, jnp.zeros((V7X_LANES - N_SMALL, seq), F32)], axis=0)
    cols_ref[0] = padded.T


def _gates(gst, batch, seq, tk):
    return pl.pallas_call(
        functools.partial(_gates_kernel, seq=seq, tk=tk),
        grid=(batch,),
        in_specs=[pl.BlockSpec((N_SMALL, seq), lambda b: (0, b))],
        out_specs=[
            pl.BlockSpec((1, N_SMALL, seq), lambda b: (b, 0, 0)),
            pl.BlockSpec((1, seq // tk, N_SMALL, tk), lambda b: (b, 0, 0, 0)),
            pl.BlockSpec((1, seq, V7X_LANES), lambda b: (b, 0, 0)),
        ],
        out_shape=[
            jax.ShapeDtypeStruct((batch, N_SMALL, seq), F32),
            jax.ShapeDtypeStruct((batch, seq // tk, N_SMALL, tk), F32),
            jax.ShapeDtypeStruct((batch, seq, V7X_LANES), F32),
        ],
        compiler_params=_params("parallel"),
        name="gates",
    )(gst)


def _fox_kernel(q_ref, k_ref, v_ref, fcol_ref, frow_ref, o_ref, *, tq):
    h = pl.program_id(1)
    qi = pl.program_id(2)
    q = q_ref[0]
    scale = FOX_HEAD_DIM ** -0.5
    lane = lax.broadcasted_iota(jnp.int32, (tq, V7X_LANES), 1)
    fq = jnp.sum(jnp.where(lane == h, fcol_ref[0], 0.0), axis=-1, keepdims=True)

    def block(kj, carry, masked):
        m, l, acc = carry
        start = pl.multiple_of(kj * tq, tq)
        k = k_ref[0, pl.ds(start, tq), :]
        v = v_ref[0, pl.ds(start, tq), :]
        fk = frow_ref[0, kj, pl.ds(h, 1), :]
        s = _dot_nt(q, k) * scale + (fq - fk)
        if masked:
            r = lax.broadcasted_iota(jnp.int32, (tq, tq), 0)
            c = lax.broadcasted_iota(jnp.int32, (tq, tq), 1)
            s = jnp.where(c <= r, s, -jnp.inf)
        m_new = jnp.maximum(m, jnp.max(s, axis=-1, keepdims=True))
        alpha = jnp.exp(m - m_new)
        p = jnp.exp(s - m_new)
        l = alpha * l + jnp.sum(p, axis=-1, keepdims=True)
        acc = alpha * acc + _dot(p.astype(BF16), v)
        return m_new, l, acc

    init = (jnp.full((tq, 1), -jnp.inf, F32), jnp.zeros((tq, 1), F32),
            jnp.zeros((tq, FOX_HEAD_DIM), F32))
    carry = lax.fori_loop(0, qi, lambda kj, c: block(kj, c, False), init)
    m, l, acc = block(qi, carry, True)
    o_ref[0] = (acc / l).astype(BF16)


FOX_TQ = 256


def _fox(proj3, krows, cols, tq=FOX_TQ):
    batch, seq, _ = proj3.shape
    return pl.pallas_call(
        functools.partial(_fox_kernel, tq=tq),
        grid=(batch, FOX_HEADS, seq // tq),
        in_specs=[
            pl.BlockSpec((1, tq, FOX_HEAD_DIM), lambda b, h, i: (b, i, h)),
            pl.BlockSpec((1, seq, FOX_HEAD_DIM), lambda b, h, i: (b, 0, FOX_HEADS + h)),
            pl.BlockSpec((1, seq, FOX_HEAD_DIM), lambda b, h, i: (b, 0, 2 * FOX_HEADS + h)),
            pl.BlockSpec((1, tq, V7X_LANES), lambda b, h, i: (b, i, 0)),
            pl.BlockSpec((1, seq // tq, N_SMALL, tq), lambda b, h, i: (b, 0, 0, 0)),
        ],
        out_specs=pl.BlockSpec((1, tq, FOX_HEAD_DIM), lambda b, h, i: (b, i, h)),
        out_shape=jax.ShapeDtypeStruct((batch, seq, FOX_WIDTH), BF16),
        compiler_params=_params("parallel", "parallel", "arbitrary"),
        name="fox_attn",
    )(proj3, proj3, proj3, cols, krows)


def _mlstm_kernel(q_ref, qh_ref, k_ref, kh_ref, v_ref, o_ref, cwq_ref, cwk_ref, cbq_ref, cbk_ref,
                  gn_ref, rows_ref, cols_ref, y_ref, c_ref, n_ref, m_ref):
    h = pl.program_id(1)
    j = pl.program_id(2)
    L = CHUNK

    @pl.when(j == 0)
    def _():
        c_ref[...] = jnp.zeros_like(c_ref)
        n_ref[...] = jnp.zeros_like(n_ref)
        m_ref[...] = jnp.zeros_like(m_ref)

    def conv_silu(x_ref, halo_ref, w_ref, b_ref):
        halo = jnp.where(j > 0, halo_ref[0].astype(F32), 0.0)
        xx = jnp.concatenate([halo, x_ref[0].astype(F32)], axis=0)
        w = w_ref[...]
        y = b_ref[...] + w[CONV_WIDTH - 1:CONV_WIDTH, :] * xx[HALO:HALO + L]
        for t in range(1, CONV_WIDTH):
            y = y + w[CONV_WIDTH - 1 - t:CONV_WIDTH - t, :] * xx[HALO - t:HALO - t + L]
        return y * jax.nn.sigmoid(y)

    q = conv_silu(q_ref, qh_ref, cwq_ref, cbq_ref)
    k = conv_silu(k_ref, kh_ref, cwk_ref, cbk_ref) * (MLSTM_HEAD_DIM ** -0.5)
    v = v_ref[0]
    qb = q.astype(BF16)
    kb = k.astype(BF16)

    lane = lax.broadcasted_iota(jnp.int32, (L, V7X_LANES), 1)
    cols = cols_ref[0]
    bcol = jnp.sum(jnp.where(lane == ROW_MLB + h, cols, 0.0), axis=-1, keepdims=True)
    icol = jnp.sum(jnp.where(lane == ROW_MLI + h, cols, 0.0), axis=-1, keepdims=True)
    brow = rows_ref[0, pl.ds(ROW_MLB + h, 1), :]
    irow = rows_ref[0, pl.ds(ROW_MLI + h, 1), :]
    m_prev = m_ref[...]

    r = lax.broadcasted_iota(jnp.int32, (L, L), 0)
    c = lax.broadcasted_iota(jnp.int32, (L, L), 1)
    dlog = jnp.where(c <= r, bcol - brow + irow, -jnp.inf)
    inter = bcol + m_prev
    m_t = jnp.maximum(inter, jnp.max(dlog, axis=-1, keepdims=True))
    w = jnp.exp(dlog - m_t)
    a = jnp.exp(inter - m_t)
    qk = _dot_nt(qb, kb) * w
    c_prev = c_ref[...]
    num = a * _dot(qb, c_prev.astype(BF16)) + _dot(qk.astype(BF16), v)
    den = a * jnp.sum(q * n_ref[...], axis=-1, keepdims=True) + jnp.sum(qk, axis=-1, keepdims=True)
    h_out = num / jnp.maximum(jnp.abs(den), jnp.exp(-m_t))

    b_end = brow[:, L - 1:L]
    g_row = b_end - brow + irow
    g_col = b_end - bcol + icol
    m_new = jnp.maximum(b_end + m_prev, jnp.max(g_row, axis=-1, keepdims=True))
    decay = jnp.exp(b_end + m_prev - m_new)
    ws = jnp.exp(g_col - m_new)
    kw = k * ws
    c_ref[...] = decay * c_prev + _dot_tn(kw.astype(BF16), v)
    n_ref[...] = decay * n_ref[...] + jnp.sum(kw, axis=0, keepdims=True)
    m_ref[...] = m_new

    y = h_out * lax.rsqrt(jnp.mean(h_out * h_out, axis=-1, keepdims=True) + EPS) * gn_ref[...]
    y_ref[0] = (y * jax.nn.sigmoid(o_ref[0].astype(F32))).astype(BF16)


def _mlstm(proj3, conv_w, conv_b, gn, rows, cols):
    batch, seq, _ = proj3.shape
    dh = MLSTM_HEAD_DIM
    nc = seq // CHUNK
    halo_per_chunk = CHUNK // HALO

    def col(base):
        return lambda b, h, j: (b, j, base // dh + h)

    def halo(base):
        return lambda b, h, j: (b, jnp.maximum(j * halo_per_chunk - 1, 0), base // dh + h)

    return pl.pallas_call(
        _mlstm_kernel,
        grid=(batch, MLSTM_HEADS, nc),
        in_specs=[
            pl.BlockSpec((1, CHUNK, dh), col(COL_LQ)),
            pl.BlockSpec((1, HALO, dh), halo(COL_LQ)),
            pl.BlockSpec((1, CHUNK, dh), col(COL_LK)),
            pl.BlockSpec((1, HALO, dh), halo(COL_LK)),
            pl.BlockSpec((1, CHUNK, dh), col(COL_LV)),
            pl.BlockSpec((1, CHUNK, dh), col(COL_LO)),
            pl.BlockSpec((CONV_WIDTH, dh), lambda b, h, j: (0, h)),
            pl.BlockSpec((CONV_WIDTH, dh), lambda b, h, j: (0, MLSTM_HEADS + h)),
            pl.BlockSpec((1, dh), lambda b, h, j: (0, h)),
            pl.BlockSpec((1, dh), lambda b, h, j: (0, MLSTM_HEADS + h)),
            pl.BlockSpec((1, dh), lambda b, h, j: (0, h)),
            pl.BlockSpec((1, N_SMALL, CHUNK), lambda b, h, j: (b, 0, j)),
            pl.BlockSpec((1, CHUNK, V7X_LANES), lambda b, h, j: (b, j, 0)),
        ],
        out_specs=pl.BlockSpec((1, CHUNK, dh), lambda b, h, j: (b, j, h)),
        out_shape=jax.ShapeDtypeStruct((batch, seq, MLSTM_WIDTH), BF16),
        scratch_shapes=[pltpu.VMEM((dh, dh), F32), pltpu.VMEM((1, dh), F32), pltpu.VMEM((1, 1), F32)],
        compiler_params=_params("parallel", "parallel", "arbitrary"),
        name="mlstm",
    )(proj3, proj3, proj3, proj3, proj3, proj3, conv_w, conv_w, conv_b, conv_b, gn, rows, cols)


def _mem_kv_kernel(mem_ref, g_ref, w_ref, o_ref):
    u = _rms(mem_ref[...], g_ref[...]).astype(BF16)
    o_ref[...] = _dot(u, w_ref[...]).astype(BF16)


def _mem_kv(mem2, g, w, bn=512):
    m = mem2.shape[0]
    n = w.shape[1]
    return pl.pallas_call(
        _mem_kv_kernel,
        grid=(n // bn,),
        in_specs=[
            pl.BlockSpec((m, D_MODEL), lambda j: (0, 0)),
            pl.BlockSpec((1, D_MODEL), lambda j: (0, 0)),
            pl.BlockSpec((D_MODEL, bn), lambda j: (0, j)),
        ],
        out_specs=pl.BlockSpec((m, bn), lambda j: (0, j)),
        out_shape=jax.ShapeDtypeStruct((m, n), BF16),
        compiler_params=_params("parallel"),
        name="mem_kv",
    )(mem2, g, w)


def _mem_attn_kernel(q_ref, k_ref, v_ref, o_ref):
    dh = MEM_HEAD_DIM
    scale = dh ** -0.5
    for hh in range(MEM_HEADS):
        sl = slice(hh * dh, (hh + 1) * dh)
        s = _dot_nt(q_ref[0, :, sl], k_ref[0, :, sl]) * scale
        p = jnp.exp(s - jnp.max(s, axis=-1, keepdims=True))
        l = jnp.sum(p, axis=-1, keepdims=True)
        o_ref[0, :, sl] = (_dot(p.astype(BF16), v_ref[0, :, sl]) / l).astype(BF16)


def _mem_attn(proj3, mkv3, tq=512):
    batch, seq, _ = proj3.shape
    mem_len = mkv3.shape[1]
    return pl.pallas_call(
        _mem_attn_kernel,
        grid=(batch, seq // tq),
        in_specs=[
            pl.BlockSpec((1, tq, MEM_WIDTH), lambda b, i: (b, i, COL_MQ // MEM_WIDTH)),
            pl.BlockSpec((1, mem_len, MEM_WIDTH), lambda b, i: (b, 0, 0)),
            pl.BlockSpec((1, mem_len, MEM_WIDTH), lambda b, i: (b, 0, 1)),
        ],
        out_specs=pl.BlockSpec((1, tq, MEM_WIDTH), lambda b, i: (b, i, 0)),
        out_shape=jax.ShapeDtypeStruct((batch, seq, MEM_WIDTH), BF16),
        compiler_params=_params("parallel", "parallel"),
        name="mem_attn",
    )(proj3, mkv3, mkv3)


def _merge_kernel(yf_ref, ym_ref, yc_ref, wf_ref, wm_ref, wc_ref, g0_ref, g1_ref, g2_ref, o_ref):
    acc = jax.nn.sigmoid(g0_ref[...].astype(F32)) * _dot(yf_ref[...], wf_ref[...])
    acc = acc + jax.nn.sigmoid(g1_ref[...].astype(F32)) * _dot(ym_ref[...], wm_ref[...])
    acc = acc + jax.nn.sigmoid(g2_ref[...].astype(F32)) * _dot(yc_ref[...], wc_ref[...])
    o_ref[...] = acc.astype(BF16)


def _merge(y_fox, y_ml, y_mem, w_f, w_m, w_c, proj, bm=512, bn=1024):
    m = y_fox.shape[0]
    kdim = y_fox.shape[1]
    y_spec = pl.BlockSpec((bm, kdim), lambda i, j: (i, 0))
    w_spec = pl.BlockSpec((kdim, bn), lambda i, j: (0, j))

    def gate_spec(branch):
        base = (COL_GATES + branch * D_MODEL) // bn
        return pl.BlockSpec((bm, bn), lambda i, j: (i, base + j))

    return pl.pallas_call(
        _merge_kernel,
        grid=(m // bm, D_MODEL // bn),
        in_specs=[y_spec, y_spec, y_spec, w_spec, w_spec, w_spec,
                  gate_spec(0), gate_spec(1), gate_spec(2)],
        out_specs=pl.BlockSpec((bm, bn), lambda i, j: (i, j)),
        out_shape=jax.ShapeDtypeStruct((m, D_MODEL), BF16),
        compiler_params=_params("parallel", "arbitrary"),
        name="merge",
    )(y_fox, y_ml, y_mem, w_f, w_m, w_c, proj, proj, proj)


def _resid_mm_kernel(a_ref, w_ref, r_ref, o_ref):
    o_ref[...] = r_ref[...] + _dot(a_ref[...], w_ref[...])


def _resid_mm(a, w, resid, name, bm=512, bn=512):
    m, kdim = a.shape
    n = w.shape[1]
    return pl.pallas_call(
        _resid_mm_kernel,
        grid=(m // bm, n // bn),
        in_specs=[
            pl.BlockSpec((bm, kdim), lambda i, j: (i, 0)),
            pl.BlockSpec((kdim, bn), lambda i, j: (0, j)),
            pl.BlockSpec((bm, bn), lambda i, j: (i, j)),
        ],
        out_specs=pl.BlockSpec((bm, bn), lambda i, j: (i, j)),
        out_shape=jax.ShapeDtypeStruct((m, n), F32),
        compiler_params=_params("parallel", "arbitrary"),
        name=name,
    )(a, w, resid)


def _ffn_in_kernel(h_ref, g_ref, wg_ref, wu_ref, o_ref, u_ref):
    @pl.when(pl.program_id(1) == 0)
    def _():
        u_ref[...] = _rms(h_ref[...], g_ref[...]).astype(BF16)

    u = u_ref[...]
    gate = _dot(u, wg_ref[...])
    up = _dot(u, wu_ref[...])
    o_ref[...] = (gate * jax.nn.sigmoid(gate) * up).astype(BF16)


def _ffn_in(h, g, w, bm=512, bn=512):
    m = h.shape[0]
    nb = D_FF // bn
    return pl.pallas_call(
        _ffn_in_kernel,
        grid=(m // bm, nb),
        in_specs=[
            pl.BlockSpec((bm, D_MODEL), lambda i, j: (i, 0)),
            pl.BlockSpec((1, D_MODEL), lambda i, j: (0, 0)),
            pl.BlockSpec((D_MODEL, bn), lambda i, j: (0, j)),
            pl.BlockSpec((D_MODEL, bn), lambda i, j: (0, nb + j)),
        ],
        out_specs=pl.BlockSpec((bm, bn), lambda i, j: (i, j)),
        out_shape=jax.ShapeDtypeStruct((m, D_FF), BF16),
        scratch_shapes=[pltpu.VMEM((bm, D_MODEL), BF16)],
        compiler_params=_params("parallel", "arbitrary"),
        name="ffn_in",
    )(h, g, w, w)


def _norm_kernel(h_ref, g_ref, o_ref):
    o_ref[...] = _rms(h_ref[...], g_ref[...])


def _final_norm(h, g, bm=512):
    m = h.shape[0]
    return pl.pallas_call(
        _norm_kernel,
        grid=(m // bm,),
        in_specs=[pl.BlockSpec((bm, D_MODEL), lambda i: (i, 0)),
                  pl.BlockSpec((1, D_MODEL), lambda i: (0, 0))],
        out_specs=pl.BlockSpec((bm, D_MODEL), lambda i: (i, 0)),
        out_shape=jax.ShapeDtypeStruct((m, D_MODEL), F32),
        compiler_params=_params("parallel"),
        name="final_norm",
    )(h, g)


def _layer(h2, mem2, batch, seq, norm_mix, w_in, b_in, conv_w, conv_b, mlstm_norm, norm_mem,
           w_mem_kv, w_br_fox, w_br_mlstm, w_br_mem, w_out, norm_ffn, w_ffn_in, w_ffn_out):
    w_big = jnp.concatenate([w_in[:, :OFF_FF], w_in[:, OFF_LQ:OFF_LI], w_in[:, OFF_LO:]], axis=1).astype(BF16)
    b_big = jnp.concatenate([b_in[:OFF_FF], b_in[OFF_LQ:OFF_LI], b_in[OFF_LO:]])[None, :]
    small_cols = (slice(OFF_FF, OFF_FF + FOX_HEADS),
                  slice(OFF_LI + MLSTM_HEADS, OFF_LI + 2 * MLSTM_HEADS),
                  slice(OFF_LI, OFF_LI + MLSTM_HEADS))
    w_small_t = jnp.concatenate([w_in[:, s] for s in small_cols], axis=1).T.astype(BF16)
    b_small_t = jnp.concatenate([b_in[s] for s in small_cols])[:, None]

    proj, gst = _in_proj(h2, norm_mix[None, :], w_big, b_big, w_small_t, b_small_t)
    rows, krows, cols = _gates(gst, batch, seq, FOX_TQ)
    proj3 = proj.reshape(batch, seq, PROJ_BIG)

    y_fox = _fox(proj3, krows, cols)
    y_ml = _mlstm(proj3, conv_w, conv_b[None, :], mlstm_norm[None, :], rows, cols)
    mkv = _mem_kv(mem2, norm_mem[None, :], w_mem_kv.astype(BF16))
    y_mem = _mem_attn(proj3, mkv.reshape(batch, -1, 2 * MEM_WIDTH))

    tokens = batch * seq
    merged = _merge(y_fox.reshape(tokens, FOX_WIDTH), y_ml.reshape(tokens, MLSTM_WIDTH),
                    y_mem.reshape(tokens, MEM_WIDTH), w_br_fox.astype(BF16), w_br_mlstm.astype(BF16),
                    w_br_mem.astype(BF16), proj)
    h2 = _resid_mm(merged, w_out.astype(BF16), h2, "out_proj")
    act = _ffn_in(h2, norm_ffn[None, :], w_ffn_in.astype(BF16))
    return _resid_mm(act, w_ffn_out.astype(BF16), h2, "ffn_out")


def kernel(x, mem, norm_mix, w_in, b_in, conv_w, conv_b, mlstm_norm, norm_mem, w_mem_kv, w_br_fox,
           w_br_mlstm, w_br_mem, w_out, norm_ffn, w_ffn_in, w_ffn_out, norm_final):
    batch, seq, d = x.shape
    assert d == D_MODEL and seq % CHUNK == 0
    h2 = x.reshape(batch * seq, d)
    mem2 = mem.reshape(batch * mem.shape[1], d)
    for l in range(norm_mix.shape[0]):
        h2 = _layer(h2, mem2, batch, seq, norm_mix[l], w_in[l], b_in[l], conv_w[l], conv_b[l],
                    mlstm_norm[l], norm_mem[l], w_mem_kv[l], w_br_fox[l], w_br_mlstm[l],
                    w_br_mem[l], w_out[l], norm_ffn[l], w_ffn_in[l], w_ffn_out[l])
    return _final_norm(h2, norm_final[None, :]).reshape(batch, seq, d)
```

```python
import functools

import jax
import jax.numpy as jnp
from jax import lax
from jax.experimental import pallas as pl
from jax.experimental.pallas import tpu as pltpu

D_MODEL = 2048
FOX_HEADS = 8
FOX_HEAD_DIM = 128
FOX_WIDTH = FOX_HEADS * FOX_HEAD_DIM
MLSTM_HEADS = 4
MLSTM_HEAD_DIM = 256
MLSTM_WIDTH = MLSTM_HEADS * MLSTM_HEAD_DIM
MEM_HEADS = 4
MEM_HEAD_DIM = 256
MEM_WIDTH = MEM_HEADS * MEM_HEAD_DIM
N_BRANCH = 3
CONV_WIDTH = 4
CHUNK = 128
HALO = 16
D_FF = 5632
EPS = 1e-6

OFF_FF = 3 * FOX_WIDTH
OFF_LQ = OFF_FF + FOX_HEADS
OFF_LI = OFF_LQ + 3 * MLSTM_WIDTH
OFF_LO = OFF_LI + 2 * MLSTM_HEADS
N_SMALL = FOX_HEADS + 2 * MLSTM_HEADS
PROJ_BIG = 3 * FOX_WIDTH + 3 * MLSTM_WIDTH + MLSTM_WIDTH + MEM_WIDTH + N_BRANCH * D_MODEL

COL_LQ = 3 * FOX_WIDTH
COL_LK = COL_LQ + MLSTM_WIDTH
COL_LV = COL_LK + MLSTM_WIDTH
COL_LO = COL_LV + MLSTM_WIDTH
COL_MQ = COL_LO + MLSTM_WIDTH
COL_GATES = COL_MQ + MEM_WIDTH

ROW_FOX = 0
ROW_MLI = FOX_HEADS
ROW_MLB = FOX_HEADS + MLSTM_HEADS

V7X_LANES = 128
V7X_VMEM_LIMIT = 56 * 1024 * 1024

F32 = jnp.float32
BF16 = jnp.bfloat16


def _params(*sem):
    return pltpu.CompilerParams(dimension_semantics=sem, vmem_limit_bytes=V7X_VMEM_LIMIT)


def _rms(x, g):
    return x * lax.rsqrt(jnp.mean(x * x, axis=-1, keepdims=True) + EPS) * g


def _dot(a, b):
    return jnp.dot(a, b, preferred_element_type=F32)


def _dot_nt(a, b):
    return lax.dot_general(a, b, (((1,), (1,)), ((), ())), preferred_element_type=F32)


def _dot_tn(a, b):
    return lax.dot_general(a, b, (((0,), (0,)), ((), ())), preferred_element_type=F32)


def _log_sigmoid(x):
    return jnp.minimum(x, 0.0) - jnp.log1p(jnp.exp(-jnp.abs(x)))


def _norm_in_kernel(x_ref, g_ref, wa_ref, wb_ref, bst_ref, u_ref, gst_ref, wt_ref):
    @pl.when(pl.program_id(0) == 0)
    def _():
        w = jnp.concatenate([wa_ref[...], wb_ref[...]], axis=1)
        wt_ref[...] = w.T.astype(BF16)

    u = _rms(x_ref[...], g_ref[...]).astype(BF16)
    u_ref[...] = u
    g_all = _dot_nt(wt_ref[...], u)
    lo = V7X_LANES + OFF_LI % V7X_LANES
    gst_ref[...] = jnp.concatenate([g_all[0:FOX_HEADS], g_all[lo:lo + 2 * MLSTM_HEADS]], axis=0) + bst_ref[...]


def _norm_in(x2, g, w_in, b_small_t, bm=512):
    m = x2.shape[0]
    assert OFF_FF % V7X_LANES == 0 and OFF_LI % V7X_LANES == FOX_HEADS
    return pl.pallas_call(
        _norm_in_kernel,
        grid=(m // bm,),
        in_specs=[
            pl.BlockSpec((bm, D_MODEL), lambda i: (i, 0)),
            pl.BlockSpec((1, D_MODEL), lambda i: (0, 0)),
            pl.BlockSpec((D_MODEL, V7X_LANES), lambda i: (0, OFF_FF // V7X_LANES)),
            pl.BlockSpec((D_MODEL, V7X_LANES), lambda i: (0, OFF_LI // V7X_LANES)),
            pl.BlockSpec((N_SMALL, 1), lambda i: (0, 0)),
        ],
        out_specs=[
            pl.BlockSpec((bm, D_MODEL), lambda i: (i, 0)),
            pl.BlockSpec((N_SMALL, bm), lambda i: (0, i)),
        ],
        out_shape=[
            jax.ShapeDtypeStruct((m, D_MODEL), BF16),
            jax.ShapeDtypeStruct((N_SMALL, m), F32),
        ],
        scratch_shapes=[pltpu.VMEM((2 * V7X_LANES, D_MODEL), BF16)],
        compiler_params=_params("arbitrary"),
        name="norm_in",
    )(x2, g, w_in, w_in, b_small_t)


CAST_ROWS = 256


def _cast_weight(dst_ref, src_fn, rows):
    def chunk(c, carry):
        sl = pl.ds(pl.multiple_of(c * CAST_ROWS, CAST_ROWS), CAST_ROWS)
        dst_ref[sl, :] = src_fn(sl).astype(BF16)
        return carry

    lax.fori_loop(0, rows // CAST_ROWS, chunk, 0)


def _in_proj_kernel(u_ref, w_ref, wx_ref, b_ref, o_ref, wb_ref, *, bn):
    j = pl.program_id(0)
    first_lq = OFF_FF // bn
    first_lo = (OFF_LI - FOX_HEADS) // bn

    def shifted(shift):
        def src(sl):
            w = jnp.concatenate([w_ref[sl, :], wx_ref[sl, :]], axis=1)
            return w[:, shift:shift + bn]
        return src

    @pl.when(pl.program_id(1) == 0)
    def _():
        @pl.when(j < first_lq)
        def _():
            _cast_weight(wb_ref, lambda sl: w_ref[sl, :], D_MODEL)

        @pl.when(jnp.logical_and(j >= first_lq, j < first_lo))
        def _():
            _cast_weight(wb_ref, shifted(FOX_HEADS), D_MODEL)

        @pl.when(j >= first_lo)
        def _():
            _cast_weight(wb_ref, shifted(N_SMALL), D_MODEL)

    mult = jnp.where(j == 0, FOX_Q_SCALE, 1.0)
    o_ref[...] = ((_dot(u_ref[...], wb_ref[...]) + b_ref[...]) * mult).astype(BF16)


def _in_proj(u, w_in, b_big, bm=1024, bn=FOX_WIDTH):
    m = u.shape[0]
    assert bn == FOX_WIDTH and OFF_FF % bn == 0 and (OFF_LI - FOX_HEADS) % bn == 0
    extra_per_tile = bn // V7X_LANES
    return pl.pallas_call(
        functools.partial(_in_proj_kernel, bn=bn),
        grid=(PROJ_BIG // bn, m // bm),
        in_specs=[
            pl.BlockSpec((bm, D_MODEL), lambda j, i: (i, 0)),
            pl.BlockSpec((D_MODEL, bn), lambda j, i: (0, j)),
            pl.BlockSpec((D_MODEL, V7X_LANES), lambda j, i: (0, (j + 1) * extra_per_tile)),
            pl.BlockSpec((1, bn), lambda j, i: (0, j)),
        ],
        out_specs=pl.BlockSpec((bm, bn), lambda j, i: (i, j)),
        out_shape=jax.ShapeDtypeStruct((m, PROJ_BIG), BF16),
        scratch_shapes=[pltpu.VMEM((D_MODEL, bn), BF16)],
        compiler_params=_params("arbitrary", "arbitrary"),
        name="in_proj",
    )(u, w_in, w_in, b_big)


def _gates_kernel(gst_ref, rows_ref, cols_ref, *, seq):
    g = gst_ref[...]
    row = lax.broadcasted_iota(jnp.int32, g.shape, 0)
    lane = lax.broadcasted_iota(jnp.int32, g.shape, 1)
    in_chunk = lane % CHUNK
    is_input_gate = jnp.logical_and(row >= ROW_MLI, row < ROW_MLB)
    local = jnp.where(is_input_gate, g, _log_sigmoid(g))
    d = 1
    while d < CHUNK:
        local = local + jnp.where(in_chunk >= d, pltpu.roll(local, d, axis=1), 0.0)
        d *= 2
    carry = pltpu.roll(jnp.where(in_chunk == CHUNK - 1, local, 0.0), 1, axis=1)
    carry = jnp.where(lane == 0, 0.0, carry)
    d = 1
    while d < CHUNK:
        carry = carry + jnp.where(in_chunk >= d, pltpu.roll(carry, d, axis=1), 0.0)
        d *= 2
    while d < seq:
        carry = carry + jnp.concatenate([jnp.zeros((N_SMALL, d), F32), carry[:, :seq - d]], axis=1)
        d *= 2
    out = jnp.where(row < ROW_MLI, local + carry, jnp.where(is_input_gate, g, local))
    rows_ref[0] = out
    padded = jnp.concatenate([out, jnp.zeros((V7X_LANES - N_SMALL, seq), F32)], axis=0)
    cols_ref[0] = padded.T


def _gates(gst, batch, seq):
    return pl.pallas_call(
        functools.partial(_gates_kernel, seq=seq),
        grid=(batch,),
        in_specs=[pl.BlockSpec((N_SMALL, seq), lambda b: (0, b))],
        out_specs=[
            pl.BlockSpec((1, N_SMALL, seq), lambda b: (b, 0, 0)),
            pl.BlockSpec((1, seq, V7X_LANES), lambda b: (b, 0, 0)),
        ],
        out_shape=[
            jax.ShapeDtypeStruct((batch, N_SMALL, seq), F32),
            jax.ShapeDtypeStruct((batch, seq, V7X_LANES), F32),
        ],
        compiler_params=_params("parallel"),
        name="gates",
    )(gst)


FOX_TQ = 1024
FOX_SUB = 128
LOG2E = 1.4426950408889634
FOX_Q_SCALE = FOX_HEAD_DIM ** -0.5 * LOG2E


def _split3(f):
    hi = f.astype(BF16).astype(F32)
    r = f - hi
    mid = r.astype(BF16).astype(F32)
    return hi, mid, r - mid


def _fox_kernel(q_ref, k_ref, v_ref, fcol_ref, o_ref, kaug_ref, vaug_ref, q2_ref, m_ref, acc_ref,
                s_ref, p_ref, alpha_ref, *, tq, sub, seq):
    h = pl.program_id(1)
    qi = pl.program_id(2)
    dh = FOX_HEAD_DIM

    def f_column(rows):
        lane = lax.broadcasted_iota(jnp.int32, (rows.shape[0], V7X_LANES), 1)
        return jnp.sum(jnp.where(lane == h, rows, 0.0), axis=-1, keepdims=True) * LOG2E

    tk = tq // 2

    @pl.when(qi == 0)
    def _():
        for c in range(seq // tk):
            sl = pl.ds(c * tk, tk)
            hi, mid, lo = _split3(f_column(fcol_ref[0, sl, :]))
            lane = lax.broadcasted_iota(jnp.int32, (tk, V7X_LANES), 1)
            aug = jnp.where(lane < 3, 1.0, jnp.where(lane == 3, -hi, jnp.where(lane == 4, -mid,
                            jnp.where(lane == 5, -lo, 0.0))))
            k2 = jnp.concatenate([k_ref[0, sl, :].astype(F32), aug], axis=1)
            kaug_ref[c] = k2.T.astype(BF16)
            vaug_ref[sl, 0:dh] = v_ref[0, sl, :]
            vaug_ref[sl, dh:2 * dh] = jnp.where(lane == 0, 1.0, 0.0).astype(BF16)

    q_start = pl.multiple_of(qi * tq, tq)
    hi, mid, lo = _split3(f_column(fcol_ref[0, pl.ds(q_start, tq), :]))
    lane = lax.broadcasted_iota(jnp.int32, (tq, V7X_LANES), 1)
    qaug = jnp.where(lane == 0, hi, jnp.where(lane == 1, mid, jnp.where(lane == 2, lo,
                     jnp.where(lane < 6, 1.0, 0.0))))
    q2_ref[:, 0:dh] = q_ref[0]
    q2_ref[:, dh:2 * dh] = qaug.astype(BF16)

    m_ref[...] = jnp.full_like(m_ref, -jnp.inf)
    acc_ref[...] = jnp.zeros_like(acc_ref)

    all_subs = [slice(r0, r0 + sub) for r0 in range(0, tq, sub)]
    low_subs = [rs for rs in all_subs if rs.start >= tk]

    def logits(t, slot, subs):
        kt = kaug_ref[t]
        for rs in subs:
            s_ref[slot, rs, :] = _dot(q2_ref[rs, :], kt)

    def softmax(slot, subs, col0=None):
        for rs in subs:
            s = s_ref[slot, rs, :]
            if col0 is not None and rs.start < col0 + tk:
                r = lax.broadcasted_iota(jnp.int32, (sub, tk), 0) + rs.start
                c = lax.broadcasted_iota(jnp.int32, (sub, tk), 1) + col0
                s = jnp.where(c <= r, s, -jnp.inf)
            m_prev = m_ref[rs, :]
            m_new = jnp.maximum(m_prev, jnp.broadcast_to(jnp.max(s, axis=-1, keepdims=True), m_prev.shape))
            alpha_ref[slot, rs, :] = jnp.exp2(m_prev - m_new)
            for c0 in range(0, tk, V7X_LANES):
                cs = slice(c0, c0 + V7X_LANES)
                p_ref[slot, rs, cs] = jnp.exp2(s[:, cs] - m_new).astype(BF16)
            m_ref[rs, :] = m_new

    def values(t, slot, subs):
        start = pl.multiple_of(t * tk, tk)
        va = vaug_ref[pl.ds(start, tk), :]
        for rs in subs:
            pv = _dot(p_ref[slot, rs, :], va)
            alpha = alpha_ref[slot, rs, :]
            for c0 in range(0, 2 * dh, V7X_LANES):
                cs = slice(c0, c0 + V7X_LANES)
                acc_ref[rs, cs] = alpha * acc_ref[rs, cs] + pv[:, cs]

    p_ref[1] = jnp.zeros((tq, tk), BF16)
    alpha_ref[1] = jnp.ones((tq, V7X_LANES), F32)
    logits(0, 0, all_subs)

    def pair(u, carry):
        t = 2 * u
        values(jnp.maximum(t - 1, 0), 1, all_subs)
        softmax(0, all_subs)
        logits(t + 1, 1, all_subs)
        values(t, 0, all_subs)
        softmax(1, all_subs)
        logits(t + 2, 0, all_subs)
        return carry

    lax.fori_loop(0, qi, pair, 0)
    t = 2 * qi
    values(jnp.maximum(t - 1, 0), 1, all_subs)
    softmax(0, all_subs, col0=0)
    logits(t + 1, 1, low_subs)
    values(t, 0, all_subs)
    softmax(1, low_subs, col0=tk)
    values(t + 1, 1, low_subs)
    o_ref[0] = (acc_ref[:, 0:dh] / acc_ref[:, dh:dh + 1]).astype(BF16)


def _fox(proj3, cols, tq=FOX_TQ, sub=FOX_SUB):
    batch, seq, _ = proj3.shape
    dh = FOX_HEAD_DIM
    tk = tq // 2
    return pl.pallas_call(
        functools.partial(_fox_kernel, tq=tq, sub=sub, seq=seq),
        grid=(batch, FOX_HEADS, seq // tq),
        in_specs=[
            pl.BlockSpec((1, tq, dh), lambda b, h, i: (b, i, h)),
            pl.BlockSpec((1, seq, dh), lambda b, h, i: (b, 0, FOX_HEADS + h)),
            pl.BlockSpec((1, seq, dh), lambda b, h, i: (b, 0, 2 * FOX_HEADS + h)),
            pl.BlockSpec((1, seq, V7X_LANES), lambda b, h, i: (b, 0, 0)),
        ],
        out_specs=pl.BlockSpec((1, tq, dh), lambda b, h, i: (b, i, h)),
        out_shape=jax.ShapeDtypeStruct((batch, seq, FOX_WIDTH), BF16),
        scratch_shapes=[pltpu.VMEM((seq // tk, 2 * dh, tk), BF16), pltpu.VMEM((seq, 2 * dh), BF16),
                        pltpu.VMEM((tq, 2 * dh), BF16), pltpu.VMEM((tq, V7X_LANES), F32),
                        pltpu.VMEM((tq, 2 * dh), F32), pltpu.VMEM((2, tq, tk), F32),
                        pltpu.VMEM((2, tq, tk), BF16), pltpu.VMEM((2, tq, V7X_LANES), F32)],
        compiler_params=_params("parallel", "parallel", "arbitrary"),
        name="fox_attn",
    )(proj3, proj3, proj3, cols)


def _mlstm_kernel(q_ref, qh_ref, k_ref, kh_ref, v_ref, o_ref, cwq_ref, cwk_ref, cbq_ref, cbk_ref,
                  gn_ref, rows_ref, cols_ref, y_ref, c_ref, n_ref, m_ref):
    h = pl.program_id(1)
    j = pl.program_id(2)
    L = CHUNK

    @pl.when(j == 0)
    def _():
        c_ref[...] = jnp.zeros_like(c_ref)
        n_ref[...] = jnp.zeros_like(n_ref)
        m_ref[...] = jnp.zeros_like(m_ref)

    def conv_silu(x_ref, halo_ref, w_ref, b_ref):
        halo = jnp.where(j > 0, halo_ref[0].astype(F32), 0.0)
        xx = jnp.concatenate([halo, x_ref[0].astype(F32)], axis=0)
        w = w_ref[...]
        y = b_ref[...] + w[CONV_WIDTH - 1:CONV_WIDTH, :] * xx[HALO:HALO + L]
        for t in range(1, CONV_WIDTH):
            y = y + w[CONV_WIDTH - 1 - t:CONV_WIDTH - t, :] * xx[HALO - t:HALO - t + L]
        return y * jax.nn.sigmoid(y)

    q = conv_silu(q_ref, qh_ref, cwq_ref, cbq_ref)
    k = conv_silu(k_ref, kh_ref, cwk_ref, cbk_ref) * (MLSTM_HEAD_DIM ** -0.5)
    v = v_ref[0]
    qb = q.astype(BF16)
    kb = k.astype(BF16)

    lane = lax.broadcasted_iota(jnp.int32, (L, V7X_LANES), 1)
    cols = cols_ref[0]
    bcol = jnp.sum(jnp.where(lane == ROW_MLB + h, cols, 0.0), axis=-1, keepdims=True)
    icol = jnp.sum(jnp.where(lane == ROW_MLI + h, cols, 0.0), axis=-1, keepdims=True)
    brow = rows_ref[0, pl.ds(ROW_MLB + h, 1), :]
    irow = rows_ref[0, pl.ds(ROW_MLI + h, 1), :]
    m_prev = m_ref[...]

    r = lax.broadcasted_iota(jnp.int32, (L, L), 0)
    c = lax.broadcasted_iota(jnp.int32, (L, L), 1)
    dlog = jnp.where(c <= r, bcol - brow + irow, -jnp.inf)
    inter = bcol + m_prev
    m_t = jnp.maximum(inter, jnp.max(dlog, axis=-1, keepdims=True))
    w = jnp.exp(dlog - m_t)
    a = jnp.exp(inter - m_t)
    qk = _dot_nt(qb, kb) * w
    c_prev = c_ref[...]
    num = a * _dot(qb, c_prev.astype(BF16)) + _dot(qk.astype(BF16), v)
    den = a * jnp.sum(q * n_ref[...], axis=-1, keepdims=True) + jnp.sum(qk, axis=-1, keepdims=True)
    h_out = num / jnp.maximum(jnp.abs(den), jnp.exp(-m_t))

    b_end = brow[:, L - 1:L]
    g_row = b_end - brow + irow
    g_col = b_end - bcol + icol
    m_new = jnp.maximum(b_end + m_prev, jnp.max(g_row, axis=-1, keepdims=True))
    decay = jnp.exp(b_end + m_prev - m_new)
    ws = jnp.exp(g_col - m_new)
    kw = k * ws
    c_ref[...] = decay * c_prev + _dot_tn(kw.astype(BF16), v)
    n_ref[...] = decay * n_ref[...] + jnp.sum(kw, axis=0, keepdims=True)
    m_ref[...] = m_new

    y = h_out * lax.rsqrt(jnp.mean(h_out * h_out, axis=-1, keepdims=True) + EPS) * gn_ref[...]
    y_ref[0] = (y * jax.nn.sigmoid(o_ref[0].astype(F32))).astype(BF16)


def _mlstm(proj3, conv_w, conv_b, gn, rows, cols):
    batch, seq, _ = proj3.shape
    dh = MLSTM_HEAD_DIM
    nc = seq // CHUNK
    halo_per_chunk = CHUNK // HALO

    def col(base):
        return lambda b, h, j: (b, j, base // dh + h)

    def halo(base):
        return lambda b, h, j: (b, jnp.maximum(j * halo_per_chunk - 1, 0), base // dh + h)

    return pl.pallas_call(
        _mlstm_kernel,
        grid=(batch, MLSTM_HEADS, nc),
        in_specs=[
            pl.BlockSpec((1, CHUNK, dh), col(COL_LQ)),
            pl.BlockSpec((1, HALO, dh), halo(COL_LQ)),
            pl.BlockSpec((1, CHUNK, dh), col(COL_LK)),
            pl.BlockSpec((1, HALO, dh), halo(COL_LK)),
            pl.BlockSpec((1, CHUNK, dh), col(COL_LV)),
            pl.BlockSpec((1, CHUNK, dh), col(COL_LO)),
            pl.BlockSpec((CONV_WIDTH, dh), lambda b, h, j: (0, h)),
            pl.BlockSpec((CONV_WIDTH, dh), lambda b, h, j: (0, MLSTM_HEADS + h)),
            pl.BlockSpec((1, dh), lambda b, h, j: (0, h)),
            pl.BlockSpec((1, dh), lambda b, h, j: (0, MLSTM_HEADS + h)),
            pl.BlockSpec((1, dh), lambda b, h, j: (0, h)),
            pl.BlockSpec((1, N_SMALL, CHUNK), lambda b, h, j: (b, 0, j)),
            pl.BlockSpec((1, CHUNK, V7X_LANES), lambda b, h, j: (b, j, 0)),
        ],
        out_specs=pl.BlockSpec((1, CHUNK, dh), lambda b, h, j: (b, j, h)),
        out_shape=jax.ShapeDtypeStruct((batch, seq, MLSTM_WIDTH), BF16),
        scratch_shapes=[pltpu.VMEM((dh, dh), F32), pltpu.VMEM((1, dh), F32), pltpu.VMEM((1, 1), F32)],
        compiler_params=_params("parallel", "parallel", "arbitrary"),
        name="mlstm",
    )(proj3, proj3, proj3, proj3, proj3, proj3, conv_w, conv_w, conv_b, conv_b, gn, rows, cols)


def _mem_kv_kernel(mem_ref, g_ref, w_ref, o_ref):
    u = _rms(mem_ref[...], g_ref[...]).astype(BF16)
    o_ref[...] = _dot(u, w_ref[...].astype(BF16)).astype(BF16)


def _mem_kv(mem2, g, w, bn=512):
    m = mem2.shape[0]
    n = w.shape[1]
    return pl.pallas_call(
        _mem_kv_kernel,
        grid=(n // bn,),
        in_specs=[
            pl.BlockSpec((m, D_MODEL), lambda j: (0, 0)),
            pl.BlockSpec((1, D_MODEL), lambda j: (0, 0)),
            pl.BlockSpec((D_MODEL, bn), lambda j: (0, j)),
        ],
        out_specs=pl.BlockSpec((m, bn), lambda j: (0, j)),
        out_shape=jax.ShapeDtypeStruct((m, n), BF16),
        compiler_params=_params("parallel"),
        name="mem_kv",
    )(mem2, g, w)


def _mem_attn_kernel(q_ref, k_ref, v_ref, o_ref):
    dh = MEM_HEAD_DIM
    scale = dh ** -0.5
    for hh in range(MEM_HEADS):
        sl = slice(hh * dh, (hh + 1) * dh)
        s = _dot_nt(q_ref[0, :, sl], k_ref[0, :, sl]) * scale
        p = jnp.exp(s - jnp.max(s, axis=-1, keepdims=True))
        l = jnp.sum(p, axis=-1, keepdims=True)
        o_ref[0, :, sl] = (_dot(p.astype(BF16), v_ref[0, :, sl]) / l).astype(BF16)


def _mem_attn(proj3, mkv3, tq=512):
    batch, seq, _ = proj3.shape
    mem_len = mkv3.shape[1]
    return pl.pallas_call(
        _mem_attn_kernel,
        grid=(batch, seq // tq),
        in_specs=[
            pl.BlockSpec((1, tq, MEM_WIDTH), lambda b, i: (b, i, COL_MQ // MEM_WIDTH)),
            pl.BlockSpec((1, mem_len, MEM_WIDTH), lambda b, i: (b, 0, 0)),
            pl.BlockSpec((1, mem_len, MEM_WIDTH), lambda b, i: (b, 0, 1)),
        ],
        out_specs=pl.BlockSpec((1, tq, MEM_WIDTH), lambda b, i: (b, i, 0)),
        out_shape=jax.ShapeDtypeStruct((batch, seq, MEM_WIDTH), BF16),
        compiler_params=_params("parallel", "parallel"),
        name="mem_attn",
    )(proj3, mkv3, mkv3)


def _merge_kernel(yf_ref, ym_ref, yc_ref, wf_ref, wm_ref, wc_ref, g0_ref, g1_ref, g2_ref, o_ref,
                  wfb_ref, wmb_ref, wcb_ref):
    @pl.when(pl.program_id(1) == 0)
    def _():
        for src, dst in ((wf_ref, wfb_ref), (wm_ref, wmb_ref), (wc_ref, wcb_ref)):
            _cast_weight(dst, lambda sl, src=src: src[sl, :], src.shape[0])

    acc = jax.nn.sigmoid(g0_ref[...].astype(F32)) * _dot(yf_ref[...], wfb_ref[...])
    acc = acc + jax.nn.sigmoid(g1_ref[...].astype(F32)) * _dot(ym_ref[...], wmb_ref[...])
    acc = acc + jax.nn.sigmoid(g2_ref[...].astype(F32)) * _dot(yc_ref[...], wcb_ref[...])
    o_ref[...] = acc.astype(BF16)


def _merge(y_fox, y_ml, y_mem, w_f, w_m, w_c, proj, bm=512, bn=512):
    m = y_fox.shape[0]
    kdim = y_fox.shape[1]
    y_spec = pl.BlockSpec((bm, kdim), lambda j, i: (i, 0))
    w_spec = pl.BlockSpec((kdim, bn), lambda j, i: (0, j))

    def gate_spec(branch):
        base = (COL_GATES + branch * D_MODEL) // bn
        return pl.BlockSpec((bm, bn), lambda j, i: (i, base + j))

    return pl.pallas_call(
        _merge_kernel,
        grid=(D_MODEL // bn, m // bm),
        in_specs=[y_spec, y_spec, y_spec, w_spec, w_spec, w_spec,
                  gate_spec(0), gate_spec(1), gate_spec(2)],
        out_specs=pl.BlockSpec((bm, bn), lambda j, i: (i, j)),
        out_shape=jax.ShapeDtypeStruct((m, D_MODEL), BF16),
        scratch_shapes=[pltpu.VMEM((kdim, bn), BF16)] * 3,
        compiler_params=_params("arbitrary", "arbitrary"),
        name="merge",
    )(y_fox, y_ml, y_mem, w_f, w_m, w_c, proj, proj, proj)


def _out_proj_kernel(a_ref, w_ref, x_ref, g_ref, h_ref, u_ref, wb_ref):
    @pl.when(pl.program_id(0) == 0)
    def _():
        _cast_weight(wb_ref, lambda sl: w_ref[sl, :], D_MODEL)

    h = x_ref[...] + _dot(a_ref[...], wb_ref[...])
    h_ref[...] = h
    u_ref[...] = _rms(h, g_ref[...]).astype(BF16)


def _out_proj(merged, w_out, x2, g_ffn, bm=256):
    m = merged.shape[0]
    return pl.pallas_call(
        _out_proj_kernel,
        grid=(m // bm,),
        in_specs=[
            pl.BlockSpec((bm, D_MODEL), lambda i: (i, 0)),
            pl.BlockSpec((D_MODEL, D_MODEL), lambda i: (0, 0), pipeline_mode=pl.Buffered(1)),
            pl.BlockSpec((bm, D_MODEL), lambda i: (i, 0)),
            pl.BlockSpec((1, D_MODEL), lambda i: (0, 0)),
        ],
        out_specs=[
            pl.BlockSpec((bm, D_MODEL), lambda i: (i, 0)),
            pl.BlockSpec((bm, D_MODEL), lambda i: (i, 0)),
        ],
        out_shape=[
            jax.ShapeDtypeStruct((m, D_MODEL), F32),
            jax.ShapeDtypeStruct((m, D_MODEL), BF16),
        ],
        scratch_shapes=[pltpu.VMEM((D_MODEL, D_MODEL), BF16)],
        compiler_params=_params("arbitrary"),
        name="out_proj",
    )(merged, w_out, x2, g_ffn)


def _ffn_in_kernel(u_ref, wg_ref, wu_ref, o_ref, wgb_ref, wub_ref):
    @pl.when(pl.program_id(1) == 0)
    def _():
        _cast_weight(wgb_ref, lambda sl: wg_ref[sl, :], D_MODEL)
        _cast_weight(wub_ref, lambda sl: wu_ref[sl, :], D_MODEL)

    u = u_ref[...]
    gate = _dot(u, wgb_ref[...])
    up = _dot(u, wub_ref[...])
    o_ref[...] = (gate * jax.nn.sigmoid(gate) * up).astype(BF16)


def _ffn_in(u, w, bm=1024, bn=512):
    m = u.shape[0]
    nb = D_FF // bn
    return pl.pallas_call(
        _ffn_in_kernel,
        grid=(nb, m // bm),
        in_specs=[
            pl.BlockSpec((bm, D_MODEL), lambda j, i: (i, 0)),
            pl.BlockSpec((D_MODEL, bn), lambda j, i: (0, j)),
            pl.BlockSpec((D_MODEL, bn), lambda j, i: (0, nb + j)),
        ],
        out_specs=pl.BlockSpec((bm, bn), lambda j, i: (i, j)),
        out_shape=jax.ShapeDtypeStruct((m, D_FF), BF16),
        scratch_shapes=[pltpu.VMEM((D_MODEL, bn), BF16)] * 2,
        compiler_params=_params("arbitrary", "arbitrary"),
        name="ffn_in",
    )(u, w, w)


def _ffn_out_kernel(a_ref, w_ref, r_ref, o_ref, wb_ref):
    @pl.when(pl.program_id(1) == 0)
    def _():
        _cast_weight(wb_ref, lambda sl: w_ref[sl, :], D_FF)

    o_ref[...] = r_ref[...] + _dot(a_ref[...], wb_ref[...])


def _ffn_out(act, w, resid, bm=512, bn=512):
    m = act.shape[0]
    return pl.pallas_call(
        _ffn_out_kernel,
        grid=(D_MODEL // bn, m // bm),
        in_specs=[
            pl.BlockSpec((bm, D_FF), lambda j, i: (i, 0)),
            pl.BlockSpec((D_FF, bn), lambda j, i: (0, j)),
            pl.BlockSpec((bm, bn), lambda j, i: (i, j)),
        ],
        out_specs=pl.BlockSpec((bm, bn), lambda j, i: (i, j)),
        out_shape=jax.ShapeDtypeStruct((m, D_MODEL), F32),
        scratch_shapes=[pltpu.VMEM((D_FF, bn), BF16)],
        compiler_params=_params("arbitrary", "arbitrary"),
        name="ffn_out",
    )(act, w, resid)


def _norm_kernel(h_ref, g_ref, o_ref):
    o_ref[...] = _rms(h_ref[...], g_ref[...])


def _final_norm(h, g, bm=512):
    m = h.shape[0]
    return pl.pallas_call(
        _norm_kernel,
        grid=(m // bm,),
        in_specs=[pl.BlockSpec((bm, D_MODEL), lambda i: (i, 0)),
                  pl.BlockSpec((1, D_MODEL), lambda i: (0, 0))],
        out_specs=pl.BlockSpec((bm, D_MODEL), lambda i: (i, 0)),
        out_shape=jax.ShapeDtypeStruct((m, D_MODEL), F32),
        compiler_params=_params("parallel"),
        name="final_norm",
    )(h, g)


def _layer(h2, mem2, batch, seq, norm_mix, w_in, b_in, conv_w, conv_b, mlstm_norm, norm_mem,
           w_mem_kv, w_br_fox, w_br_mlstm, w_br_mem, w_out, norm_ffn, w_ffn_in, w_ffn_out):
    b_big = jnp.concatenate([b_in[:OFF_FF], b_in[OFF_LQ:OFF_LI], b_in[OFF_LO:]])[None, :]
    b_small_t = jnp.concatenate([b_in[OFF_FF:OFF_LQ], b_in[OFF_LI:OFF_LO]])[:, None]

    u, gst = _norm_in(h2, norm_mix[None, :], w_in, b_small_t)
    proj = _in_proj(u, w_in, b_big)
    rows, cols = _gates(gst, batch, seq)
    proj3 = proj.reshape(batch, seq, PROJ_BIG)

    y_fox = _fox(proj3, cols)
    y_ml = _mlstm(proj3, conv_w, conv_b[None, :], mlstm_norm[None, :], rows, cols)
    mkv = _mem_kv(mem2, norm_mem[None, :], w_mem_kv)
    y_mem = _mem_attn(proj3, mkv.reshape(batch, -1, 2 * MEM_WIDTH))

    tokens = batch * seq
    merged = _merge(y_fox.reshape(tokens, FOX_WIDTH), y_ml.reshape(tokens, MLSTM_WIDTH),
                    y_mem.reshape(tokens, MEM_WIDTH), w_br_fox, w_br_mlstm, w_br_mem, proj)
    h2, u_ffn = _out_proj(merged, w_out, h2, norm_ffn[None, :])
    act = _ffn_in(u_ffn, w_ffn_in)
    return _ffn_out(act, w_ffn_out, h2)


def kernel(x, mem, norm_mix, w_in, b_in, conv_w, conv_b, mlstm_norm, norm_mem, w_mem_kv, w_br_fox,
           w_br_mlstm, w_br_mem, w_out, norm_ffn, w_ffn_in, w_ffn_out, norm_final):
    batch, seq, d = x.shape
    assert d == D_MODEL and seq % CHUNK == 0
    h2 = x.reshape(batch * seq, d)
    mem2 = mem.reshape(batch * mem.shape[1], d)
    for l in range(norm_mix.shape[0]):
        h2 = _layer(h2, mem2, batch, seq, norm_mix[l], w_in[l], b_in[l], conv_w[l], conv_b[l],
                    mlstm_norm[l], norm_mem[l], w_mem_kv[l], w_br_fox[l], w_br_mlstm[l],
                    w_br_mem[l], w_out[l], norm_ffn[l], w_ffn_in[l], w_ffn_out[l])
    return _final_norm(h2, norm_final[None, :]).reshape(batch, seq, d)
```

```python
import functools

import jax
import jax.numpy as jnp
from jax import lax
from jax.experimental import pallas as pl
from jax.experimental.pallas import tpu as pltpu

D_MODEL = 2048
FOX_HEADS = 8
FOX_HEAD_DIM = 128
FOX_WIDTH = FOX_HEADS * FOX_HEAD_DIM
MLSTM_HEADS = 4
MLSTM_HEAD_DIM = 256
MLSTM_WIDTH = MLSTM_HEADS * MLSTM_HEAD_DIM
MEM_HEADS = 4
MEM_HEAD_DIM = 256
MEM_WIDTH = MEM_HEADS * MEM_HEAD_DIM
N_BRANCH = 3
CONV_WIDTH = 4
CHUNK = 128
HALO = 16
D_FF = 5632
EPS = 1e-6

OFF_FF = 3 * FOX_WIDTH
OFF_LQ = OFF_FF + FOX_HEADS
OFF_LI = OFF_LQ + 3 * MLSTM_WIDTH
OFF_LO = OFF_LI + 2 * MLSTM_HEADS
N_SMALL = FOX_HEADS + 2 * MLSTM_HEADS
PROJ_BIG = 3 * FOX_WIDTH + 3 * MLSTM_WIDTH + MLSTM_WIDTH + MEM_WIDTH + N_BRANCH * D_MODEL

COL_LQ = 3 * FOX_WIDTH
COL_LK = COL_LQ + MLSTM_WIDTH
COL_LV = COL_LK + MLSTM_WIDTH
COL_LO = COL_LV + MLSTM_WIDTH
COL_MQ = COL_LO + MLSTM_WIDTH
COL_GATES = COL_MQ + MEM_WIDTH

ROW_FOX = 0
ROW_MLI = FOX_HEADS
ROW_MLB = FOX_HEADS + MLSTM_HEADS

V7X_LANES = 128
V7X_VMEM_LIMIT = 56 * 1024 * 1024

F32 = jnp.float32
BF16 = jnp.bfloat16


def _params(*sem):
    return pltpu.CompilerParams(dimension_semantics=sem, vmem_limit_bytes=V7X_VMEM_LIMIT)


def _rms(x, g):
    return x * lax.rsqrt(jnp.mean(x * x, axis=-1, keepdims=True) + EPS) * g


def _dot(a, b):
    return jnp.dot(a, b, preferred_element_type=F32)


def _dot_nt(a, b):
    return lax.dot_general(a, b, (((1,), (1,)), ((), ())), preferred_element_type=F32)


def _dot_tn(a, b):
    return lax.dot_general(a, b, (((0,), (0,)), ((), ())), preferred_element_type=F32)


def _log_sigmoid(x):
    return jnp.minimum(x, 0.0) - jnp.log1p(jnp.exp(-jnp.abs(x)))


SUBLANES = 8


def _norm_in_kernel(x_ref, g_ref, wf_ref, wl_ref, bst_ref, u_ref, gst_ref, wt_ref):
    @pl.when(pl.program_id(0) == 0)
    def _():
        wt_ref[...] = jnp.concatenate([wf_ref[...], wl_ref[...]], axis=0).astype(BF16)

    u = _rms(x_ref[...], g_ref[...]).astype(BF16)
    u_ref[...] = u
    gst_ref[...] = _dot_nt(wt_ref[...], u) + bst_ref[...]


def _norm_in(x2, g, w_in_t, b_small_t, bm=512):
    m = x2.shape[0]
    assert FOX_HEADS == SUBLANES and 2 * MLSTM_HEADS == SUBLANES
    return pl.pallas_call(
        _norm_in_kernel,
        grid=(m // bm,),
        in_specs=[
            pl.BlockSpec((bm, D_MODEL), lambda i: (i, 0)),
            pl.BlockSpec((1, D_MODEL), lambda i: (0, 0)),
            pl.BlockSpec((SUBLANES, D_MODEL), lambda i: (OFF_FF // SUBLANES, 0)),
            pl.BlockSpec((SUBLANES, D_MODEL), lambda i: (OFF_LI // SUBLANES, 0)),
            pl.BlockSpec((N_SMALL, 1), lambda i: (0, 0)),
        ],
        out_specs=[
            pl.BlockSpec((bm, D_MODEL), lambda i: (i, 0)),
            pl.BlockSpec((N_SMALL, bm), lambda i: (0, i)),
        ],
        out_shape=[
            jax.ShapeDtypeStruct((m, D_MODEL), BF16),
            jax.ShapeDtypeStruct((N_SMALL, m), F32),
        ],
        scratch_shapes=[pltpu.VMEM((N_SMALL, D_MODEL), BF16)],
        compiler_params=_params("arbitrary"),
        name="norm_in",
    )(x2, g, w_in_t, w_in_t, b_small_t)


CAST_ROWS = 256


def _cast_weight(dst_ref, src_fn, rows):
    def chunk(c, carry):
        sl = pl.ds(pl.multiple_of(c * CAST_ROWS, CAST_ROWS), CAST_ROWS)
        dst_ref[sl, :] = src_fn(sl).astype(BF16)
        return carry

    lax.fori_loop(0, rows // CAST_ROWS, chunk, 0)


def _in_proj_kernel(u_ref, w_ref, wx_ref, b_ref, o_ref, wb_ref, *, bn):
    j = pl.program_id(0)
    first_lq = OFF_FF // bn
    first_lo = (OFF_LI - FOX_HEADS) // bn

    def cast_shifted(shift):
        def body():
            def chunk(c, carry):
                src = pl.ds(pl.multiple_of(c * CAST_ROWS + shift, SUBLANES), CAST_ROWS)
                dst = pl.ds(pl.multiple_of(c * CAST_ROWS, CAST_ROWS), CAST_ROWS)
                wb_ref[dst, :] = w_ref[src, :].astype(BF16)
                return carry

            lax.fori_loop(0, bn // CAST_ROWS - 1, chunk, 0)
            last = bn - CAST_ROWS
            tail = jnp.concatenate([w_ref[last + shift:bn, :], wx_ref[0:shift, :]], axis=0)
            wb_ref[last:bn, :] = tail.astype(BF16)
        return body

    @pl.when(pl.program_id(1) == 0)
    def _():
        @pl.when(j < first_lq)
        def _():
            _cast_weight(wb_ref, lambda sl: w_ref[sl, :], bn)

        pl.when(jnp.logical_and(j >= first_lq, j < first_lo))(cast_shifted(FOX_HEADS))
        pl.when(j >= first_lo)(cast_shifted(N_SMALL))

    mult = jnp.where(j == 0, FOX_Q_SCALE, 1.0)
    o_ref[...] = ((_dot_nt(u_ref[...], wb_ref[...]) + b_ref[...]) * mult).astype(BF16)


def _in_proj(u, w_in_t, b_big, bm=1024, bn=FOX_WIDTH):
    m = u.shape[0]
    assert bn == FOX_WIDTH and OFF_FF % bn == 0 and (OFF_LI - FOX_HEADS) % bn == 0
    extra_per_tile = bn // N_SMALL
    return pl.pallas_call(
        functools.partial(_in_proj_kernel, bn=bn),
        grid=(PROJ_BIG // bn, m // bm),
        in_specs=[
            pl.BlockSpec((bm, D_MODEL), lambda j, i: (i, 0)),
            pl.BlockSpec((bn, D_MODEL), lambda j, i: (j, 0)),
            pl.BlockSpec((N_SMALL, D_MODEL), lambda j, i: ((j + 1) * extra_per_tile, 0)),
            pl.BlockSpec((1, bn), lambda j, i: (0, j)),
        ],
        out_specs=pl.BlockSpec((bm, bn), lambda j, i: (i, j)),
        out_shape=jax.ShapeDtypeStruct((m, PROJ_BIG), BF16),
        scratch_shapes=[pltpu.VMEM((bn, D_MODEL), BF16)],
        compiler_params=_params("arbitrary", "arbitrary"),
        name="in_proj",
    )(u, w_in_t, w_in_t, b_big)


def _gates_kernel(gst_ref, rows_ref, cols_ref, *, seq):
    g = gst_ref[...]
    row = lax.broadcasted_iota(jnp.int32, g.shape, 0)
    lane = lax.broadcasted_iota(jnp.int32, g.shape, 1)
    in_chunk = lane % CHUNK
    is_input_gate = jnp.logical_and(row >= ROW_MLI, row < ROW_MLB)
    local = jnp.where(is_input_gate, g, _log_sigmoid(g))
    d = 1
    while d < CHUNK:
        local = local + jnp.where(in_chunk >= d, pltpu.roll(local, d, axis=1), 0.0)
        d *= 2
    carry = pltpu.roll(jnp.where(in_chunk == CHUNK - 1, local, 0.0), 1, axis=1)
    carry = jnp.where(lane == 0, 0.0, carry)
    d = 1
    while d < CHUNK:
        carry = carry + jnp.where(in_chunk >= d, pltpu.roll(carry, d, axis=1), 0.0)
        d *= 2
    while d < seq:
        carry = carry + jnp.concatenate([jnp.zeros((N_SMALL, d), F32), carry[:, :seq - d]], axis=1)
        d *= 2
    out = jnp.where(row < ROW_MLI, local + carry, jnp.where(is_input_gate, g, local))
    rows_ref[0] = out
    padded = jnp.concatenate([out, jnp.zeros((V7X_LANES - N_SMALL, seq), F32)], axis=0)
    cols_ref[0] = padded.T


def _gates(gst, batch, seq):
    return pl.pallas_call(
        functools.partial(_gates_kernel, seq=seq),
        grid=(batch,),
        in_specs=[pl.BlockSpec((N_SMALL, seq), lambda b: (0, b))],
        out_specs=[
            pl.BlockSpec((1, N_SMALL, seq), lambda b: (b, 0, 0)),
            pl.BlockSpec((1, seq, V7X_LANES), lambda b: (b, 0, 0)),
        ],
        out_shape=[
            jax.ShapeDtypeStruct((batch, N_SMALL, seq), F32),
            jax.ShapeDtypeStruct((batch, seq, V7X_LANES), F32),
        ],
        compiler_params=_params("parallel"),
        name="gates",
    )(gst)


FOX_TQ = 1024
FOX_SUB = 128
LOG2E = 1.4426950408889634
FOX_Q_SCALE = FOX_HEAD_DIM ** -0.5 * LOG2E


def _split3(f):
    hi = f.astype(BF16).astype(F32)
    r = f - hi
    mid = r.astype(BF16).astype(F32)
    return hi, mid, r - mid


def _fox_kernel(q_ref, k_ref, v_ref, fcol_ref, o_ref, kaug_ref, vaug_ref, q2_ref, m_ref, acc_ref,
                s_ref, p_ref, alpha_ref, *, tq, sub, seq):
    h = pl.program_id(1)
    qi = pl.program_id(2)
    dh = FOX_HEAD_DIM

    def f_column(rows):
        lane = lax.broadcasted_iota(jnp.int32, (rows.shape[0], V7X_LANES), 1)
        return jnp.sum(jnp.where(lane == h, rows, 0.0), axis=-1, keepdims=True) * LOG2E

    tk = tq // 2

    @pl.when(qi == 0)
    def _():
        for c in range(seq // tk):
            sl = pl.ds(c * tk, tk)
            hi, mid, lo = _split3(f_column(fcol_ref[0, sl, :]))
            lane = lax.broadcasted_iota(jnp.int32, (tk, V7X_LANES), 1)
            aug = jnp.where(lane < 3, 1.0, jnp.where(lane == 3, -hi, jnp.where(lane == 4, -mid,
                            jnp.where(lane == 5, -lo, 0.0))))
            k2 = jnp.concatenate([k_ref[0, sl, :].astype(F32), aug], axis=1)
            kaug_ref[c] = k2.T.astype(BF16)
            vaug_ref[sl, 0:dh] = v_ref[0, sl, :]
            vaug_ref[sl, dh:2 * dh] = jnp.where(lane == 0, 1.0, 0.0).astype(BF16)

    q_start = pl.multiple_of(qi * tq, tq)
    hi, mid, lo = _split3(f_column(fcol_ref[0, pl.ds(q_start, tq), :]))
    lane = lax.broadcasted_iota(jnp.int32, (tq, V7X_LANES), 1)
    qaug = jnp.where(lane == 0, hi, jnp.where(lane == 1, mid, jnp.where(lane == 2, lo,
                     jnp.where(lane < 6, 1.0, 0.0))))
    q2_ref[:, 0:dh] = q_ref[0]
    q2_ref[:, dh:2 * dh] = qaug.astype(BF16)

    m_ref[...] = jnp.full_like(m_ref, -jnp.inf)
    acc_ref[...] = jnp.zeros_like(acc_ref)

    all_subs = [slice(r0, r0 + sub) for r0 in range(0, tq, sub)]
    low_subs = [rs for rs in all_subs if rs.start >= tk]

    def logits(t, slot, subs):
        kt = kaug_ref[t]
        for rs in subs:
            s_ref[slot, rs, :] = _dot(q2_ref[rs, :], kt)

    def softmax(slot, subs, col0=None):
        for rs in subs:
            s = s_ref[slot, rs, :]
            if col0 is not None and rs.start < col0 + tk:
                r = lax.broadcasted_iota(jnp.int32, (sub, tk), 0) + rs.start
                c = lax.broadcasted_iota(jnp.int32, (sub, tk), 1) + col0
                s = jnp.where(c <= r, s, -jnp.inf)
            m_prev = m_ref[rs, :]
            m_new = jnp.maximum(m_prev, jnp.broadcast_to(jnp.max(s, axis=-1, keepdims=True), m_prev.shape))
            alpha_ref[slot, rs, :] = jnp.exp2(m_prev - m_new)
            for c0 in range(0, tk, V7X_LANES):
                cs = slice(c0, c0 + V7X_LANES)
                p_ref[slot, rs, cs] = jnp.exp2(s[:, cs] - m_new).astype(BF16)
            m_ref[rs, :] = m_new

    def values(t, slot, subs):
        start = pl.multiple_of(t * tk, tk)
        va = vaug_ref[pl.ds(start, tk), :]
        for rs in subs:
            pv = _dot(p_ref[slot, rs, :], va)
            alpha = alpha_ref[slot, rs, :]
            for c0 in range(0, 2 * dh, V7X_LANES):
                cs = slice(c0, c0 + V7X_LANES)
                acc_ref[rs, cs] = alpha * acc_ref[rs, cs] + pv[:, cs]

    p_ref[1] = jnp.zeros((tq, tk), BF16)
    alpha_ref[1] = jnp.ones((tq, V7X_LANES), F32)
    logits(0, 0, all_subs)

    def pair(u, carry):
        t = 2 * u
        values(jnp.maximum(t - 1, 0), 1, all_subs)
        softmax(0, all_subs)
        logits(t + 1, 1, all_subs)
        values(t, 0, all_subs)
        softmax(1, all_subs)
        logits(t + 2, 0, all_subs)
        return carry

    lax.fori_loop(0, qi, pair, 0)
    t = 2 * qi
    values(jnp.maximum(t - 1, 0), 1, all_subs)
    softmax(0, all_subs, col0=0)
    logits(t + 1, 1, low_subs)
    values(t, 0, all_subs)
    softmax(1, low_subs, col0=tk)
    values(t + 1, 1, low_subs)
    o_ref[0] = (acc_ref[:, 0:dh] / acc_ref[:, dh:dh + 1]).astype(BF16)


def _fox(proj3, cols, tq=FOX_TQ, sub=FOX_SUB):
    batch, seq, _ = proj3.shape
    dh = FOX_HEAD_DIM
    tk = tq // 2
    return pl.pallas_call(
        functools.partial(_fox_kernel, tq=tq, sub=sub, seq=seq),
        grid=(batch, FOX_HEADS, seq // tq),
        in_specs=[
            pl.BlockSpec((1, tq, dh), lambda b, h, i: (b, i, h)),
            pl.BlockSpec((1, seq, dh), lambda b, h, i: (b, 0, FOX_HEADS + h)),
            pl.BlockSpec((1, seq, dh), lambda b, h, i: (b, 0, 2 * FOX_HEADS + h)),
            pl.BlockSpec((1, seq, V7X_LANES), lambda b, h, i: (b, 0, 0)),
        ],
        out_specs=pl.BlockSpec((1, tq, dh), lambda b, h, i: (b, i, h)),
        out_shape=jax.ShapeDtypeStruct((batch, seq, FOX_WIDTH), BF16),
        scratch_shapes=[pltpu.VMEM((seq // tk, 2 * dh, tk), BF16), pltpu.VMEM((seq, 2 * dh), BF16),
                        pltpu.VMEM((tq, 2 * dh), BF16), pltpu.VMEM((tq, V7X_LANES), F32),
                        pltpu.VMEM((tq, 2 * dh), F32), pltpu.VMEM((2, tq, tk), F32),
                        pltpu.VMEM((2, tq, tk), BF16), pltpu.VMEM((2, tq, V7X_LANES), F32)],
        compiler_params=_params("parallel", "parallel", "arbitrary"),
        name="fox_attn",
    )(proj3, proj3, proj3, cols)


def _mlstm_kernel(q_ref, qh_ref, k_ref, kh_ref, v_ref, o_ref, cw_ref, cb_ref, gn_ref, rows_ref,
                  cols_ref, y_ref, c_ref, n_ref, m_ref, *, batch):
    j = pl.program_id(0)
    L = CHUNK
    dh = MLSTM_HEAD_DIM
    width = MLSTM_WIDTH

    @pl.when(j == 0)
    def _():
        c_ref[...] = jnp.zeros_like(c_ref)
        n_ref[...] = jnp.zeros_like(n_ref)
        m_ref[...] = jnp.zeros_like(m_ref)

    def conv_silu(x, halo, w, bias):
        halo = jnp.where(j > 0, halo.astype(F32), 0.0)
        xx = jnp.concatenate([halo, x.astype(F32)], axis=0)
        y = bias + w[CONV_WIDTH - 1:CONV_WIDTH, :] * xx[HALO:HALO + L]
        for t in range(1, CONV_WIDTH):
            y = y + w[CONV_WIDTH - 1 - t:CONV_WIDTH - t, :] * xx[HALO - t:HALO - t + L]
        return y * jax.nn.sigmoid(y)

    r = lax.broadcasted_iota(jnp.int32, (L, L), 0)
    c = lax.broadcasted_iota(jnp.int32, (L, L), 1)
    causal = c <= r
    for b in range(batch):
        q_all = conv_silu(q_ref[b], qh_ref[b], cw_ref[:, 0:width], cb_ref[:, 0:width])
        k_all = conv_silu(k_ref[b], kh_ref[b], cw_ref[:, width:2 * width],
                          cb_ref[:, width:2 * width]) * (dh ** -0.5)
        cols = cols_ref[b]
        for h in range(MLSTM_HEADS):
            hs = slice(h * dh, (h + 1) * dh)
            st = b * MLSTM_HEADS + h
            q = q_all[:, hs]
            k = k_all[:, hs]
            v = v_ref[b, :, hs]
            qb = q.astype(BF16)
            bcol = cols[:, ROW_MLB + h:ROW_MLB + h + 1]
            icol = cols[:, ROW_MLI + h:ROW_MLI + h + 1]
            brow = rows_ref[b, ROW_MLB + h:ROW_MLB + h + 1, :]
            irow = rows_ref[b, ROW_MLI + h:ROW_MLI + h + 1, :]
            m_prev = m_ref[st]

            dlog = jnp.where(causal, bcol - brow + irow, -jnp.inf)
            inter = bcol + m_prev
            m_t = jnp.maximum(inter, jnp.max(dlog, axis=-1, keepdims=True))
            w = jnp.exp(dlog - m_t)
            a = jnp.exp(inter - m_t)
            qk = _dot_nt(qb, k.astype(BF16)) * w
            c_prev = c_ref[st]
            num = a * _dot(qb, c_prev.astype(BF16)) + _dot(qk.astype(BF16), v)
            den = (a * jnp.sum(q * n_ref[st], axis=-1, keepdims=True)
                   + jnp.sum(qk, axis=-1, keepdims=True))
            h_out = num / jnp.maximum(jnp.abs(den), jnp.exp(-m_t))

            b_end = brow[:, L - 1:L]
            g_row = b_end - brow + irow
            g_col = b_end - bcol + icol
            m_new = jnp.maximum(b_end + m_prev, jnp.max(g_row, axis=-1, keepdims=True))
            decay = jnp.exp(b_end + m_prev - m_new)
            kw = k * jnp.exp(g_col - m_new)
            c_ref[st] = decay * c_prev + _dot_tn(kw.astype(BF16), v)
            n_ref[st] = decay * n_ref[st] + jnp.sum(kw, axis=0, keepdims=True)
            m_ref[st] = m_new

            y = h_out * lax.rsqrt(jnp.mean(h_out * h_out, axis=-1, keepdims=True) + EPS) * gn_ref[:, hs]
            y_ref[b, :, hs] = (y * jax.nn.sigmoid(o_ref[b, :, hs].astype(F32))).astype(BF16)


def _mlstm(proj3, conv_w, conv_b, gn, rows, cols):
    batch, seq, _ = proj3.shape
    dh = MLSTM_HEAD_DIM
    width = MLSTM_WIDTH
    nc = seq // CHUNK
    halo_per_chunk = CHUNK // HALO
    chains = batch * MLSTM_HEADS

    def col(base):
        return lambda j: (0, j, base // width)

    def halo(base):
        return lambda j: (0, jnp.maximum(j * halo_per_chunk - 1, 0), base // width)

    return pl.pallas_call(
        functools.partial(_mlstm_kernel, batch=batch),
        grid=(nc,),
        in_specs=[
            pl.BlockSpec((batch, CHUNK, width), col(COL_LQ)),
            pl.BlockSpec((batch, HALO, width), halo(COL_LQ)),
            pl.BlockSpec((batch, CHUNK, width), col(COL_LK)),
            pl.BlockSpec((batch, HALO, width), halo(COL_LK)),
            pl.BlockSpec((batch, CHUNK, width), col(COL_LV)),
            pl.BlockSpec((batch, CHUNK, width), col(COL_LO)),
            pl.BlockSpec((CONV_WIDTH, 2 * width), lambda j: (0, 0)),
            pl.BlockSpec((1, 2 * width), lambda j: (0, 0)),
            pl.BlockSpec((1, width), lambda j: (0, 0)),
            pl.BlockSpec((batch, N_SMALL, CHUNK), lambda j: (0, 0, j)),
            pl.BlockSpec((batch, CHUNK, V7X_LANES), lambda j: (0, j, 0)),
        ],
        out_specs=pl.BlockSpec((batch, CHUNK, width), lambda j: (0, j, 0)),
        out_shape=jax.ShapeDtypeStruct((batch, seq, width), BF16),
        scratch_shapes=[pltpu.VMEM((chains, dh, dh), F32), pltpu.VMEM((chains, 1, dh), F32),
                        pltpu.VMEM((chains, 1, 1), F32)],
        compiler_params=_params("arbitrary"),
        name="mlstm",
    )(proj3, proj3, proj3, proj3, proj3, proj3, conv_w, conv_b, gn, rows, cols)


def _mem_kv_kernel(mem_ref, g_ref, w_ref, o_ref):
    u = _rms(mem_ref[...], g_ref[...]).astype(BF16)
    o_ref[...] = _dot(u, w_ref[...].astype(BF16)).astype(BF16)


def _mem_kv(mem2, g, w, bn=512):
    m = mem2.shape[0]
    n = w.shape[1]
    return pl.pallas_call(
        _mem_kv_kernel,
        grid=(n // bn,),
        in_specs=[
            pl.BlockSpec((m, D_MODEL), lambda j: (0, 0)),
            pl.BlockSpec((1, D_MODEL), lambda j: (0, 0)),
            pl.BlockSpec((D_MODEL, bn), lambda j: (0, j)),
        ],
        out_specs=pl.BlockSpec((m, bn), lambda j: (0, j)),
        out_shape=jax.ShapeDtypeStruct((m, n), BF16),
        compiler_params=_params("parallel"),
        name="mem_kv",
    )(mem2, g, w)


def _mem_attn_kernel(q_ref, k_ref, v_ref, o_ref):
    dh = MEM_HEAD_DIM
    scale = dh ** -0.5
    for hh in range(MEM_HEADS):
        sl = slice(hh * dh, (hh + 1) * dh)
        s = _dot_nt(q_ref[0, :, sl], k_ref[0, :, sl]) * scale
        p = jnp.exp(s - jnp.max(s, axis=-1, keepdims=True))
        l = jnp.sum(p, axis=-1, keepdims=True)
        o_ref[0, :, sl] = (_dot(p.astype(BF16), v_ref[0, :, sl]) / l).astype(BF16)


def _mem_attn(proj3, mkv3, tq=512):
    batch, seq, _ = proj3.shape
    mem_len = mkv3.shape[1]
    return pl.pallas_call(
        _mem_attn_kernel,
        grid=(batch, seq // tq),
        in_specs=[
            pl.BlockSpec((1, tq, MEM_WIDTH), lambda b, i: (b, i, COL_MQ // MEM_WIDTH)),
            pl.BlockSpec((1, mem_len, MEM_WIDTH), lambda b, i: (b, 0, 0)),
            pl.BlockSpec((1, mem_len, MEM_WIDTH), lambda b, i: (b, 0, 1)),
        ],
        out_specs=pl.BlockSpec((1, tq, MEM_WIDTH), lambda b, i: (b, i, 0)),
        out_shape=jax.ShapeDtypeStruct((batch, seq, MEM_WIDTH), BF16),
        compiler_params=_params("parallel", "parallel"),
        name="mem_attn",
    )(proj3, mkv3, mkv3)


def _merge_kernel(yf_ref, ym_ref, yc_ref, wf_ref, wm_ref, wc_ref, g0_ref, g1_ref, g2_ref, o_ref,
                  wfb_ref, wmb_ref, wcb_ref):
    @pl.when(pl.program_id(1) == 0)
    def _():
        for src, dst in ((wf_ref, wfb_ref), (wm_ref, wmb_ref), (wc_ref, wcb_ref)):
            _cast_weight(dst, lambda sl, src=src: src[sl, :], src.shape[0])

    acc = jax.nn.sigmoid(g0_ref[...].astype(F32)) * _dot(yf_ref[...], wfb_ref[...])
    acc = acc + jax.nn.sigmoid(g1_ref[...].astype(F32)) * _dot(ym_ref[...], wmb_ref[...])
    acc = acc + jax.nn.sigmoid(g2_ref[...].astype(F32)) * _dot(yc_ref[...], wcb_ref[...])
    o_ref[...] = acc.astype(BF16)


def _merge(y_fox, y_ml, y_mem, w_f, w_m, w_c, proj, bm=512, bn=512):
    m = y_fox.shape[0]
    kdim = y_fox.shape[1]
    y_spec = pl.BlockSpec((bm, kdim), lambda j, i: (i, 0))
    w_spec = pl.BlockSpec((kdim, bn), lambda j, i: (0, j))

    def gate_spec(branch):
        base = (COL_GATES + branch * D_MODEL) // bn
        return pl.BlockSpec((bm, bn), lambda j, i: (i, base + j))

    return pl.pallas_call(
        _merge_kernel,
        grid=(D_MODEL // bn, m // bm),
        in_specs=[y_spec, y_spec, y_spec, w_spec, w_spec, w_spec,
                  gate_spec(0), gate_spec(1), gate_spec(2)],
        out_specs=pl.BlockSpec((bm, bn), lambda j, i: (i, j)),
        out_shape=jax.ShapeDtypeStruct((m, D_MODEL), BF16),
        scratch_shapes=[pltpu.VMEM((kdim, bn), BF16)] * 3,
        compiler_params=_params("arbitrary", "arbitrary"),
        name="merge",
    )(y_fox, y_ml, y_mem, w_f, w_m, w_c, proj, proj, proj)


def _out_proj_kernel(a_ref, w_ref, x_ref, g_ref, h_ref, u_ref, wb_ref):
    @pl.when(pl.program_id(0) == 0)
    def _():
        _cast_weight(wb_ref, lambda sl: w_ref[sl, :], D_MODEL)

    h = x_ref[...] + _dot(a_ref[...], wb_ref[...])
    h_ref[...] = h
    u_ref[...] = _rms(h, g_ref[...]).astype(BF16)


def _out_proj(merged, w_out, x2, g_ffn, bm=256):
    m = merged.shape[0]
    return pl.pallas_call(
        _out_proj_kernel,
        grid=(m // bm,),
        in_specs=[
            pl.BlockSpec((bm, D_MODEL), lambda i: (i, 0)),
            pl.BlockSpec((D_MODEL, D_MODEL), lambda i: (0, 0), pipeline_mode=pl.Buffered(1)),
            pl.BlockSpec((bm, D_MODEL), lambda i: (i, 0)),
            pl.BlockSpec((1, D_MODEL), lambda i: (0, 0)),
        ],
        out_specs=[
            pl.BlockSpec((bm, D_MODEL), lambda i: (i, 0)),
            pl.BlockSpec((bm, D_MODEL), lambda i: (i, 0)),
        ],
        out_shape=[
            jax.ShapeDtypeStruct((m, D_MODEL), F32),
            jax.ShapeDtypeStruct((m, D_MODEL), BF16),
        ],
        scratch_shapes=[pltpu.VMEM((D_MODEL, D_MODEL), BF16)],
        compiler_params=_params("arbitrary"),
        name="out_proj",
    )(merged, w_out, x2, g_ffn)


def _ffn_in_kernel(u_ref, wg_ref, wu_ref, o_ref, wgb_ref, wub_ref):
    @pl.when(pl.program_id(1) == 0)
    def _():
        _cast_weight(wgb_ref, lambda sl: wg_ref[sl, :], D_MODEL)
        _cast_weight(wub_ref, lambda sl: wu_ref[sl, :], D_MODEL)

    u = u_ref[...]
    gate = _dot(u, wgb_ref[...])
    up = _dot(u, wub_ref[...])
    o_ref[...] = (gate * jax.nn.sigmoid(gate) * up).astype(BF16)


def _ffn_in(u, w, bm=1024, bn=512):
    m = u.shape[0]
    nb = D_FF // bn
    return pl.pallas_call(
        _ffn_in_kernel,
        grid=(nb, m // bm),
        in_specs=[
            pl.BlockSpec((bm, D_MODEL), lambda j, i: (i, 0)),
            pl.BlockSpec((D_MODEL, bn), lambda j, i: (0, j)),
            pl.BlockSpec((D_MODEL, bn), lambda j, i: (0, nb + j)),
        ],
        out_specs=pl.BlockSpec((bm, bn), lambda j, i: (i, j)),
        out_shape=jax.ShapeDtypeStruct((m, D_FF), BF16),
        scratch_shapes=[pltpu.VMEM((D_MODEL, bn), BF16)] * 2,
        compiler_params=_params("arbitrary", "arbitrary"),
        name="ffn_in",
    )(u, w, w)


def _ffn_out_kernel(a_ref, w_ref, r_ref, o_ref, wb_ref):
    @pl.when(pl.program_id(1) == 0)
    def _():
        _cast_weight(wb_ref, lambda sl: w_ref[sl, :], D_FF)

    o_ref[...] = r_ref[...] + _dot(a_ref[...], wb_ref[...])


def _ffn_out(act, w, resid, bm=512, bn=512):
    m = act.shape[0]
    return pl.pallas_call(
        _ffn_out_kernel,
        grid=(D_MODEL // bn, m // bm),
        in_specs=[
            pl.BlockSpec((bm, D_FF), lambda j, i: (i, 0)),
            pl.BlockSpec((D_FF, bn), lambda j, i: (0, j)),
            pl.BlockSpec((bm, bn), lambda j, i: (i, j)),
        ],
        out_specs=pl.BlockSpec((bm, bn), lambda j, i: (i, j)),
        out_shape=jax.ShapeDtypeStruct((m, D_MODEL), F32),
        scratch_shapes=[pltpu.VMEM((D_FF, bn), BF16)],
        compiler_params=_params("arbitrary", "arbitrary"),
        name="ffn_out",
    )(act, w, resid)


def _norm_kernel(h_ref, g_ref, o_ref):
    o_ref[...] = _rms(h_ref[...], g_ref[...])


def _final_norm(h, g, bm=512):
    m = h.shape[0]
    return pl.pallas_call(
        _norm_kernel,
        grid=(m // bm,),
        in_specs=[pl.BlockSpec((bm, D_MODEL), lambda i: (i, 0)),
                  pl.BlockSpec((1, D_MODEL), lambda i: (0, 0))],
        out_specs=pl.BlockSpec((bm, D_MODEL), lambda i: (i, 0)),
        out_shape=jax.ShapeDtypeStruct((m, D_MODEL), F32),
        compiler_params=_params("parallel"),
        name="final_norm",
    )(h, g)


def _layer(h2, mem2, batch, seq, norm_mix, w_in, b_in, conv_w, conv_b, mlstm_norm, norm_mem,
           w_mem_kv, w_br_fox, w_br_mlstm, w_br_mem, w_out, norm_ffn, w_ffn_in, w_ffn_out):
    b_big = jnp.concatenate([b_in[:OFF_FF], b_in[OFF_LQ:OFF_LI], b_in[OFF_LO:]])[None, :]
    b_small_t = jnp.concatenate([b_in[OFF_FF:OFF_LQ], b_in[OFF_LI:OFF_LO]])[:, None]

    w_in_t = w_in.T
    u, gst = _norm_in(h2, norm_mix[None, :], w_in_t, b_small_t)
    proj = _in_proj(u, w_in_t, b_big)
    rows, cols = _gates(gst, batch, seq)
    proj3 = proj.reshape(batch, seq, PROJ_BIG)

    y_fox = _fox(proj3, cols)
    y_ml = _mlstm(proj3, conv_w, conv_b[None, :], mlstm_norm[None, :], rows, cols)
    mkv = _mem_kv(mem2, norm_mem[None, :], w_mem_kv)
    y_mem = _mem_attn(proj3, mkv.reshape(batch, -1, 2 * MEM_WIDTH))

    tokens = batch * seq
    merged = _merge(y_fox.reshape(tokens, FOX_WIDTH), y_ml.reshape(tokens, MLSTM_WIDTH),
                    y_mem.reshape(tokens, MEM_WIDTH), w_br_fox, w_br_mlstm, w_br_mem, proj)
    h2, u_ffn = _out_proj(merged, w_out, h2, norm_ffn[None, :])
    act = _ffn_in(u_ffn, w_ffn_in)
    return _ffn_out(act, w_ffn_out, h2)


def kernel(x, mem, norm_mix, w_in, b_in, conv_w, conv_b, mlstm_norm, norm_mem, w_mem_kv, w_br_fox,
           w_br_mlstm, w_br_mem, w_out, norm_ffn, w_ffn_in, w_ffn_out, norm_final):
    batch, seq, d = x.shape
    assert d == D_MODEL and seq % CHUNK == 0
    h2 = x.reshape(batch * seq, d)
    mem2 = mem.reshape(batch * mem.shape[1], d)
    for l in range(norm_mix.shape[0]):
        h2 = _layer(h2, mem2, batch, seq, norm_mix[l], w_in[l], b_in[l], conv_w[l], conv_b[l],
                    mlstm_norm[l], norm_mem[l], w_mem_kv[l], w_br_fox[l], w_br_mlstm[l],
                    w_br_mem[l], w_out[l], norm_ffn[l], w_ffn_in[l], w_ffn_out[l])
    return _final_norm(h2, norm_final[None, :]).reshape(batch, seq, d)
```

```python
import functools

import jax
import jax.numpy as jnp
from jax import lax
from jax.experimental import pallas as pl
from jax.experimental.pallas import tpu as pltpu

D_MODEL = 2048
FOX_HEADS = 8
FOX_HEAD_DIM = 128
FOX_WIDTH = FOX_HEADS * FOX_HEAD_DIM
MLSTM_HEADS = 4
MLSTM_HEAD_DIM = 256
MLSTM_WIDTH = MLSTM_HEADS * MLSTM_HEAD_DIM
MEM_HEADS = 4
MEM_HEAD_DIM = 256
MEM_WIDTH = MEM_HEADS * MEM_HEAD_DIM
N_BRANCH = 3
CONV_WIDTH = 4
CHUNK = 128
D_FF = 5632
EPS = 1e-6

OFF_FF = 3 * FOX_WIDTH
OFF_LQ = OFF_FF + FOX_HEADS
OFF_LI = OFF_LQ + 3 * MLSTM_WIDTH
OFF_LO = OFF_LI + 2 * MLSTM_HEADS
N_SMALL = FOX_HEADS + 2 * MLSTM_HEADS
PROJ_BIG = 3 * FOX_WIDTH + 3 * MLSTM_WIDTH + MLSTM_WIDTH + MEM_WIDTH + N_BRANCH * D_MODEL

COL_LQ = 3 * FOX_WIDTH
COL_LK = COL_LQ + MLSTM_WIDTH
COL_LV = COL_LK + MLSTM_WIDTH
COL_LO = COL_LV + MLSTM_WIDTH
COL_MQ = COL_LO + MLSTM_WIDTH
COL_GATES = COL_MQ + MEM_WIDTH

ROW_FOX = 0
ROW_MLI = FOX_HEADS
ROW_MLB = FOX_HEADS + MLSTM_HEADS

V7X_LANES = 128
V7X_VMEM_LIMIT = 56 * 1024 * 1024

F32 = jnp.float32
BF16 = jnp.bfloat16


def _params(*sem):
    return pltpu.CompilerParams(dimension_semantics=sem, vmem_limit_bytes=V7X_VMEM_LIMIT)


def _rms(x, g):
    return x * lax.rsqrt(jnp.mean(x * x, axis=-1, keepdims=True) + EPS) * g


def _dot(a, b):
    return jnp.dot(a, b, preferred_element_type=F32)


def _dot_nt(a, b):
    return lax.dot_general(a, b, (((1,), (1,)), ((), ())), preferred_element_type=F32)


def _dot_tn(a, b):
    return lax.dot_general(a, b, (((0,), (0,)), ((), ())), preferred_element_type=F32)


def _log_sigmoid(x):
    return jnp.minimum(x, 0.0) - jnp.log1p(jnp.exp(-jnp.abs(x)))


SUBLANES = 8


def _norm_in_kernel(x_ref, g_ref, wf_ref, wl_ref, bst_ref, u_ref, gst_ref, wt_ref):
    @pl.when(pl.program_id(0) == 0)
    def _():
        wt_ref[...] = jnp.concatenate([wf_ref[...], wl_ref[...]], axis=0).astype(BF16)

    u = _rms(x_ref[...], g_ref[...]).astype(BF16)
    u_ref[...] = u
    gst_ref[...] = _dot_nt(wt_ref[...], u) + bst_ref[...]


def _norm_in(x2, g, w_in_t, b_small_t, bm=512):
    m = x2.shape[0]
    assert FOX_HEADS == SUBLANES and 2 * MLSTM_HEADS == SUBLANES
    return pl.pallas_call(
        _norm_in_kernel,
        grid=(m // bm,),
        in_specs=[
            pl.BlockSpec((bm, D_MODEL), lambda i: (i, 0)),
            pl.BlockSpec((1, D_MODEL), lambda i: (0, 0)),
            pl.BlockSpec((SUBLANES, D_MODEL), lambda i: (OFF_FF // SUBLANES, 0)),
            pl.BlockSpec((SUBLANES, D_MODEL), lambda i: (OFF_LI // SUBLANES, 0)),
            pl.BlockSpec((N_SMALL, 1), lambda i: (0, 0)),
        ],
        out_specs=[
            pl.BlockSpec((bm, D_MODEL), lambda i: (i, 0)),
            pl.BlockSpec((N_SMALL, bm), lambda i: (0, i)),
        ],
        out_shape=[
            jax.ShapeDtypeStruct((m, D_MODEL), BF16),
            jax.ShapeDtypeStruct((N_SMALL, m), F32),
        ],
        scratch_shapes=[pltpu.VMEM((N_SMALL, D_MODEL), BF16)],
        compiler_params=_params("arbitrary"),
        name="norm_in",
    )(x2, g, w_in_t, w_in_t, b_small_t)


CAST_ROWS = 256
EPILOGUE_ROWS = 256


def _cast_weight(dst_ref, src_fn, rows):
    def chunk(c, carry):
        sl = pl.ds(pl.multiple_of(c * CAST_ROWS, CAST_ROWS), CAST_ROWS)
        dst_ref[sl, :] = src_fn(sl).astype(BF16)
        return carry

    lax.fori_loop(0, rows // CAST_ROWS, chunk, 0)


def _in_proj_kernel(u_ref, w_ref, wx_ref, b_ref, cw_ref, cb_ref, o_ref, wb_ref, xs_ref,
                    *, bn, tiles_per_seq):
    j = pl.program_id(0)
    first_lq = OFF_FF // bn
    first_lo = (OFF_LI - FOX_HEADS) // bn

    def cast_shifted(shift):
        def body():
            def chunk(c, carry):
                src = pl.ds(pl.multiple_of(c * CAST_ROWS + shift, SUBLANES), CAST_ROWS)
                dst = pl.ds(pl.multiple_of(c * CAST_ROWS, CAST_ROWS), CAST_ROWS)
                wb_ref[dst, :] = w_ref[src, :].astype(BF16)
                return carry

            lax.fori_loop(0, bn // CAST_ROWS - 1, chunk, 0)
            last = bn - CAST_ROWS
            tail = jnp.concatenate([w_ref[last + shift:bn, :], wx_ref[0:shift, :]], axis=0)
            wb_ref[last:bn, :] = tail.astype(BF16)
        return body

    @pl.when(pl.program_id(1) == 0)
    def _():
        @pl.when(j < first_lq)
        def _():
            _cast_weight(wb_ref, lambda sl: w_ref[sl, :], bn)

        pl.when(jnp.logical_and(j >= first_lq, j < first_lo))(cast_shifted(FOX_HEADS))
        pl.when(j >= first_lo)(cast_shifted(N_SMALL))

    i = pl.program_id(1)
    bm = u_ref.shape[0]
    t_lq, t_lk, t_lo, t_mq = (c // bn for c in (COL_LQ, COL_LK, COL_LO, COL_MQ))
    is_conv = jnp.logical_or(j == t_lq, j == t_lk)
    is_gate = jnp.logical_or(j == t_lo, j > t_mq)

    row_tiles = [slice(r0, r0 + EPILOGUE_ROWS) for r0 in range(0, bm, EPILOGUE_ROWS)]

    def raw(rs):
        return _dot_nt(u_ref[rs, :], wb_ref[...]) + b_ref[...]

    @pl.when(jnp.logical_not(jnp.logical_or(is_conv, is_gate)))
    def _():
        mult = jnp.where(j == 0, FOX_Q_SCALE, jnp.where(j == t_mq, MEM_HEAD_DIM ** -0.5, 1.0))
        for rs in row_tiles:
            o_ref[rs, :] = (raw(rs) * mult).astype(BF16)

    @pl.when(is_gate)
    def _():
        for rs in row_tiles:
            o_ref[rs, :] = jax.nn.sigmoid(raw(rs)).astype(BF16)

    @pl.when(is_conv)
    def _():
        cw = cw_ref[...]
        mult = jnp.where(j == t_lk, MLSTM_HEAD_DIM ** -0.5, 1.0)
        @pl.when(i % tiles_per_seq == 0)
        def _():
            xs_ref[0, 0:SUBLANES, :] = jnp.zeros((SUBLANES, bn), F32)

        for n, rs in enumerate(row_tiles):
            cur, nxt = n % 2, (n + 1) % 2
            x = raw(rs)
            xs_ref[cur, SUBLANES:SUBLANES + EPILOGUE_ROWS, :] = x
            xs_ref[nxt, 0:SUBLANES, :] = x[EPILOGUE_ROWS - SUBLANES:EPILOGUE_ROWS, :]
            y = cb_ref[...] + cw[CONV_WIDTH - 1:CONV_WIDTH, :] * x
            for t in range(1, CONV_WIDTH):
                y = y + (cw[CONV_WIDTH - 1 - t:CONV_WIDTH - t, :]
                         * xs_ref[cur, SUBLANES - t:SUBLANES - t + EPILOGUE_ROWS, :])
            o_ref[rs, :] = (y * jax.nn.sigmoid(y) * mult).astype(BF16)


def _in_proj(u, w_in_t, b_big, conv_w, conv_b, seq, bm=1024, bn=FOX_WIDTH):
    m = u.shape[0]
    assert bn == FOX_WIDTH and OFF_FF % bn == 0 and (OFF_LI - FOX_HEADS) % bn == 0
    assert seq % bm == 0 and MLSTM_WIDTH == bn and CONV_WIDTH - 1 <= SUBLANES
    extra_per_tile = bn // N_SMALL
    t_lq = COL_LQ // bn
    return pl.pallas_call(
        functools.partial(_in_proj_kernel, bn=bn, tiles_per_seq=seq // bm),
        grid=(PROJ_BIG // bn, m // bm),
        in_specs=[
            pl.BlockSpec((bm, D_MODEL), lambda j, i: (i, 0)),
            pl.BlockSpec((bn, D_MODEL), lambda j, i: (j, 0)),
            pl.BlockSpec((N_SMALL, D_MODEL), lambda j, i: ((j + 1) * extra_per_tile, 0)),
            pl.BlockSpec((1, bn), lambda j, i: (0, j)),
            pl.BlockSpec((CONV_WIDTH, bn), lambda j, i: (0, jnp.clip(j - t_lq, 0, 1))),
            pl.BlockSpec((1, bn), lambda j, i: (0, jnp.clip(j - t_lq, 0, 1))),
        ],
        out_specs=pl.BlockSpec((bm, bn), lambda j, i: (i, j)),
        out_shape=jax.ShapeDtypeStruct((m, PROJ_BIG), BF16),
        scratch_shapes=[pltpu.VMEM((bn, D_MODEL), BF16),
                        pltpu.VMEM((2, SUBLANES + EPILOGUE_ROWS, bn), F32)],
        compiler_params=_params("arbitrary", "arbitrary"),
        name="in_proj",
    )(u, w_in_t, w_in_t, b_big, conv_w, conv_b)


def _gates_kernel(gst_ref, rows_ref, cols_ref, *, seq):
    g = gst_ref[...]
    row = lax.broadcasted_iota(jnp.int32, g.shape, 0)
    lane = lax.broadcasted_iota(jnp.int32, g.shape, 1)
    in_chunk = lane % CHUNK
    is_input_gate = jnp.logical_and(row >= ROW_MLI, row < ROW_MLB)
    local = jnp.where(is_input_gate, g, _log_sigmoid(g))
    d = 1
    while d < CHUNK:
        local = local + jnp.where(in_chunk >= d, pltpu.roll(local, d, axis=1), 0.0)
        d *= 2
    carry = pltpu.roll(jnp.where(in_chunk == CHUNK - 1, local, 0.0), 1, axis=1)
    carry = jnp.where(lane == 0, 0.0, carry)
    d = 1
    while d < CHUNK:
        carry = carry + jnp.where(in_chunk >= d, pltpu.roll(carry, d, axis=1), 0.0)
        d *= 2
    while d < seq:
        carry = carry + jnp.concatenate([jnp.zeros((N_SMALL, d), F32), carry[:, :seq - d]], axis=1)
        d *= 2
    out = jnp.where(row < ROW_MLI, local + carry, jnp.where(is_input_gate, g, local))
    rows_ref[0] = out
    padded = jnp.concatenate([out, jnp.zeros((V7X_LANES - N_SMALL, seq), F32)], axis=0)
    cols_ref[0] = padded.T


def _gates(gst, batch, seq):
    return pl.pallas_call(
        functools.partial(_gates_kernel, seq=seq),
        grid=(batch,),
        in_specs=[pl.BlockSpec((N_SMALL, seq), lambda b: (0, b))],
        out_specs=[
            pl.BlockSpec((1, N_SMALL, seq), lambda b: (b, 0, 0)),
            pl.BlockSpec((1, seq, V7X_LANES), lambda b: (b, 0, 0)),
        ],
        out_shape=[
            jax.ShapeDtypeStruct((batch, N_SMALL, seq), F32),
            jax.ShapeDtypeStruct((batch, seq, V7X_LANES), F32),
        ],
        compiler_params=_params("parallel"),
        name="gates",
    )(gst)


FOX_TQ = 1024
FOX_SUB = 128
LOG2E = 1.4426950408889634
FOX_Q_SCALE = FOX_HEAD_DIM ** -0.5 * LOG2E


def _split3(f):
    hi = f.astype(BF16).astype(F32)
    r = f - hi
    mid = r.astype(BF16).astype(F32)
    return hi, mid, r - mid


def _fox_kernel(q_ref, k_ref, v_ref, fcol_ref, o_ref, kaug_ref, vaug_ref, q2_ref, m_ref, acc_ref,
                s_ref, p_ref, alpha_ref, *, tq, sub, seq):
    h = pl.program_id(1)
    qi = pl.program_id(2)
    dh = FOX_HEAD_DIM

    def f_column(rows):
        lane = lax.broadcasted_iota(jnp.int32, (rows.shape[0], V7X_LANES), 1)
        return jnp.sum(jnp.where(lane == h, rows, 0.0), axis=-1, keepdims=True) * LOG2E

    tk = tq // 2

    @pl.when(qi == 0)
    def _():
        for c in range(seq // tk):
            sl = pl.ds(c * tk, tk)
            hi, mid, lo = _split3(f_column(fcol_ref[0, sl, :]))
            lane = lax.broadcasted_iota(jnp.int32, (tk, V7X_LANES), 1)
            aug = jnp.where(lane < 3, 1.0, jnp.where(lane == 3, -hi, jnp.where(lane == 4, -mid,
                            jnp.where(lane == 5, -lo, 0.0))))
            k2 = jnp.concatenate([k_ref[0, sl, :].astype(F32), aug], axis=1)
            kaug_ref[c] = k2.T.astype(BF16)
            vaug_ref[sl, 0:dh] = v_ref[0, sl, :]
            vaug_ref[sl, dh:2 * dh] = jnp.where(lane == 0, 1.0, 0.0).astype(BF16)

    q_start = pl.multiple_of(qi * tq, tq)
    hi, mid, lo = _split3(f_column(fcol_ref[0, pl.ds(q_start, tq), :]))
    lane = lax.broadcasted_iota(jnp.int32, (tq, V7X_LANES), 1)
    qaug = jnp.where(lane == 0, hi, jnp.where(lane == 1, mid, jnp.where(lane == 2, lo,
                     jnp.where(lane < 6, 1.0, 0.0))))
    q2_ref[:, 0:dh] = q_ref[0]
    q2_ref[:, dh:2 * dh] = qaug.astype(BF16)

    m_ref[...] = jnp.full_like(m_ref, -jnp.inf)
    acc_ref[...] = jnp.zeros_like(acc_ref)

    all_subs = [slice(r0, r0 + sub) for r0 in range(0, tq, sub)]
    low_subs = [rs for rs in all_subs if rs.start >= tk]

    def logits(t, slot, subs):
        kt = kaug_ref[t]
        for rs in subs:
            s_ref[slot, rs, :] = _dot(q2_ref[rs, :], kt)

    def softmax(slot, subs, col0=None):
        for rs in subs:
            s = s_ref[slot, rs, :]
            if col0 is not None and rs.start < col0 + tk:
                r = lax.broadcasted_iota(jnp.int32, (sub, tk), 0) + rs.start
                c = lax.broadcasted_iota(jnp.int32, (sub, tk), 1) + col0
                s = jnp.where(c <= r, s, -jnp.inf)
            m_prev = m_ref[rs, :]
            m_new = jnp.maximum(m_prev, jnp.broadcast_to(jnp.max(s, axis=-1, keepdims=True), m_prev.shape))
            alpha_ref[slot, rs, :] = jnp.exp2(m_prev - m_new)
            for c0 in range(0, tk, V7X_LANES):
                cs = slice(c0, c0 + V7X_LANES)
                p_ref[slot, rs, cs] = jnp.exp2(s[:, cs] - m_new).astype(BF16)
            m_ref[rs, :] = m_new

    def values(t, slot, subs):
        start = pl.multiple_of(t * tk, tk)
        va = vaug_ref[pl.ds(start, tk), :]
        for rs in subs:
            pv = _dot(p_ref[slot, rs, :], va)
            alpha = alpha_ref[slot, rs, :]
            for c0 in range(0, 2 * dh, V7X_LANES):
                cs = slice(c0, c0 + V7X_LANES)
                acc_ref[rs, cs] = alpha * acc_ref[rs, cs] + pv[:, cs]

    p_ref[1] = jnp.zeros((tq, tk), BF16)
    alpha_ref[1] = jnp.ones((tq, V7X_LANES), F32)
    logits(0, 0, all_subs)

    def pair(u, carry):
        t = 2 * u
        values(jnp.maximum(t - 1, 0), 1, all_subs)
        softmax(0, all_subs)
        logits(t + 1, 1, all_subs)
        values(t, 0, all_subs)
        softmax(1, all_subs)
        logits(t + 2, 0, all_subs)
        return carry

    lax.fori_loop(0, qi, pair, 0)
    t = 2 * qi
    values(jnp.maximum(t - 1, 0), 1, all_subs)
    softmax(0, all_subs, col0=0)
    logits(t + 1, 1, low_subs)
    values(t, 0, all_subs)
    softmax(1, low_subs, col0=tk)
    values(t + 1, 1, low_subs)
    o_ref[0] = (acc_ref[:, 0:dh] / acc_ref[:, dh:dh + 1]).astype(BF16)


def _fox(proj3, cols, tq=FOX_TQ, sub=FOX_SUB):
    batch, seq, _ = proj3.shape
    dh = FOX_HEAD_DIM
    tk = tq // 2
    return pl.pallas_call(
        functools.partial(_fox_kernel, tq=tq, sub=sub, seq=seq),
        grid=(batch, FOX_HEADS, seq // tq),
        in_specs=[
            pl.BlockSpec((1, tq, dh), lambda b, h, i: (b, i, h)),
            pl.BlockSpec((1, seq, dh), lambda b, h, i: (b, 0, FOX_HEADS + h)),
            pl.BlockSpec((1, seq, dh), lambda b, h, i: (b, 0, 2 * FOX_HEADS + h)),
            pl.BlockSpec((1, seq, V7X_LANES), lambda b, h, i: (b, 0, 0)),
        ],
        out_specs=pl.BlockSpec((1, tq, dh), lambda b, h, i: (b, i, h)),
        out_shape=jax.ShapeDtypeStruct((batch, seq, FOX_WIDTH), BF16),
        scratch_shapes=[pltpu.VMEM((seq // tk, 2 * dh, tk), BF16), pltpu.VMEM((seq, 2 * dh), BF16),
                        pltpu.VMEM((tq, 2 * dh), BF16), pltpu.VMEM((tq, V7X_LANES), F32),
                        pltpu.VMEM((tq, 2 * dh), F32), pltpu.VMEM((2, tq, tk), F32),
                        pltpu.VMEM((2, tq, tk), BF16), pltpu.VMEM((2, tq, V7X_LANES), F32)],
        compiler_params=_params("parallel", "parallel", "arbitrary"),
        name="fox_attn",
    )(proj3, proj3, proj3, cols)


def _mlstm_kernel(q_ref, k_ref, v_ref, o_ref, gn_ref, rows_ref, cols_ref, y_ref, c_ref, n_ref, m_ref,
                  *, batch):
    j = pl.program_id(0)
    L = CHUNK
    dh = MLSTM_HEAD_DIM

    @pl.when(j == 0)
    def _():
        c_ref[...] = jnp.zeros_like(c_ref)
        n_ref[...] = jnp.zeros_like(n_ref)
        m_ref[...] = jnp.zeros_like(m_ref)

    lanes = V7X_LANES
    assert L == lanes and dh % lanes == 0
    halves = [slice(c0, c0 + lanes) for c0 in range(0, dh, lanes)]
    r = lax.broadcasted_iota(jnp.int32, (L, L), 0)
    c = lax.broadcasted_iota(jnp.int32, (L, L), 1)
    causal = c <= r
    chains = [(b, h) for b in range(batch) for h in range(MLSTM_HEADS)]

    def rep(x):
        return jnp.broadcast_to(x, (L, lanes))

    sel_r = lax.broadcasted_iota(jnp.int32, (3 * lanes, 2 * lanes), 0) % lanes
    sel_c = lax.broadcasted_iota(jnp.int32, (3 * lanes, 2 * lanes), 1)
    pieces = []
    for b in range(batch):
        hi, mid, lo = _split3(cols_ref[b])
        pieces.append(jnp.concatenate([hi, mid, lo], axis=1).astype(BF16))

    st1 = []
    for b, h in chains:
        hs = slice(h * dh, (h + 1) * dh)
        st = b * MLSTM_HEADS + h
        sel = sel_r == jnp.where(sel_c < lanes, ROW_MLB + h, ROW_MLI + h)
        bi = _dot(pieces[b], jnp.where(sel, 1.0, 0.0).astype(BF16))
        bcol, icol = bi[:, 0:lanes], bi[:, lanes:2 * lanes]
        qb = q_ref[b, :, hs]
        kb = k_ref[b, :, hs]
        brow = rows_ref[b, ROW_MLB + h:ROW_MLB + h + 1, :]
        irow = rows_ref[b, ROW_MLI + h:ROW_MLI + h + 1, :]
        m_prev = m_ref[st]
        c_prev = c_ref[st]
        qk_raw = _dot_nt(qb, kb)
        qc = _dot(qb, c_prev.astype(BF16))

        dlog = jnp.where(causal, bcol - brow + irow, -jnp.inf)
        inter = bcol + m_prev
        m_t = jnp.maximum(inter, rep(jnp.max(dlog, axis=-1, keepdims=True)))
        w = jnp.exp(dlog - m_t)
        a = jnp.exp(inter - m_t)
        b_end = brow[:, L - 1:L]
        m_new = jnp.maximum(b_end + m_prev, jnp.max(b_end - brow + irow, axis=-1, keepdims=True))
        decay = jnp.exp(b_end + m_prev - m_new)
        ws = jnp.exp(b_end - bcol + icol - m_new)
        qn = rep(jnp.sum(qb.astype(F32) * n_ref[st], axis=-1, keepdims=True))
        st1.append((hs, st, kb, c_prev, qk_raw, qc, m_t, w, a, m_new, decay, ws, qn))

    st2 = []
    for (b, h), (hs, st, kb, c_prev, qk_raw, qc, m_t, w, a, m_new, decay, ws, qn) in zip(chains, st1):
        v = v_ref[b, :, hs]
        qk = qk_raw * w
        k = kb.astype(F32)
        kw = jnp.concatenate([k[:, cs] * ws for cs in halves], axis=1)
        pv = _dot(qk.astype(BF16), v)
        upd = _dot_tn(kw.astype(BF16), v)
        st2.append((qk, kw, pv, upd))

    for (b, h), s1, (qk, kw, pv, upd) in zip(chains, st1, st2):
        hs, st, kb, c_prev, qk_raw, qc, m_t, w, a, m_new, decay, ws, qn = s1
        den = a * qn + rep(jnp.sum(qk, axis=-1, keepdims=True))
        inv = 1.0 / jnp.maximum(jnp.abs(den), jnp.exp(-m_t))
        h_parts = [(a * qc[:, cs] + pv[:, cs]) * inv for cs in halves]
        sq = h_parts[0] * h_parts[0]
        for hp in h_parts[1:]:
            sq = sq + hp * hp
        rms = lax.rsqrt(rep(jnp.sum(sq, axis=-1, keepdims=True)) * (1.0 / dh) + EPS)
        for cs, hp in zip(halves, h_parts):
            oc = slice(hs.start + cs.start, hs.start + cs.stop)
            y = hp * rms * gn_ref[:, oc] * o_ref[b, :, oc].astype(F32)
            y_ref[b, :, oc] = y.astype(BF16)
        c_ref[st] = decay * c_prev + upd
        n_ref[st] = decay * n_ref[st] + jnp.sum(kw, axis=0, keepdims=True)
        m_ref[st] = m_new


def _mlstm(proj3, gn, rows, cols):
    batch, seq, _ = proj3.shape
    dh = MLSTM_HEAD_DIM
    width = MLSTM_WIDTH
    nc = seq // CHUNK
    chains = batch * MLSTM_HEADS

    def col(base):
        return lambda j: (0, j, base // width)

    return pl.pallas_call(
        functools.partial(_mlstm_kernel, batch=batch),
        grid=(nc,),
        in_specs=[
            pl.BlockSpec((batch, CHUNK, width), col(COL_LQ)),
            pl.BlockSpec((batch, CHUNK, width), col(COL_LK)),
            pl.BlockSpec((batch, CHUNK, width), col(COL_LV)),
            pl.BlockSpec((batch, CHUNK, width), col(COL_LO)),
            pl.BlockSpec((1, width), lambda j: (0, 0)),
            pl.BlockSpec((batch, N_SMALL, CHUNK), lambda j: (0, 0, j)),
            pl.BlockSpec((batch, CHUNK, V7X_LANES), lambda j: (0, j, 0)),
        ],
        out_specs=pl.BlockSpec((batch, CHUNK, width), lambda j: (0, j, 0)),
        out_shape=jax.ShapeDtypeStruct((batch, seq, width), BF16),
        scratch_shapes=[pltpu.VMEM((chains, dh, dh), F32), pltpu.VMEM((chains, 1, dh), F32),
                        pltpu.VMEM((chains, 1, 1), F32)],
        compiler_params=_params("arbitrary"),
        name="mlstm",
    )(proj3, proj3, proj3, proj3, gn, rows, cols)


def _mem_kv_kernel(mem_ref, g_ref, w_ref, o_ref):
    u = _rms(mem_ref[...], g_ref[...]).astype(BF16)
    o_ref[...] = _dot(u, w_ref[...].astype(BF16)).astype(BF16)


def _mem_kv(mem2, g, w, bn=512):
    m = mem2.shape[0]
    n = w.shape[1]
    return pl.pallas_call(
        _mem_kv_kernel,
        grid=(n // bn,),
        in_specs=[
            pl.BlockSpec((m, D_MODEL), lambda j: (0, 0)),
            pl.BlockSpec((1, D_MODEL), lambda j: (0, 0)),
            pl.BlockSpec((D_MODEL, bn), lambda j: (0, j)),
        ],
        out_specs=pl.BlockSpec((m, bn), lambda j: (0, j)),
        out_shape=jax.ShapeDtypeStruct((m, n), BF16),
        compiler_params=_params("parallel"),
        name="mem_kv",
    )(mem2, g, w)


def _mem_attn_kernel(q_ref, k_ref, v_ref, o_ref):
    dh = MEM_HEAD_DIM
    for hh in range(MEM_HEADS):
        sl = slice(hh * dh, (hh + 1) * dh)
        s = _dot_nt(q_ref[0, :, sl], k_ref[0, :, sl])
        p = jnp.exp(s - jnp.max(s, axis=-1, keepdims=True))
        l = jnp.sum(p, axis=-1, keepdims=True)
        o_ref[0, :, sl] = (_dot(p.astype(BF16), v_ref[0, :, sl]) / l).astype(BF16)


def _mem_attn(proj3, mkv3, tq=512):
    batch, seq, _ = proj3.shape
    mem_len = mkv3.shape[1]
    return pl.pallas_call(
        _mem_attn_kernel,
        grid=(batch, seq // tq),
        in_specs=[
            pl.BlockSpec((1, tq, MEM_WIDTH), lambda b, i: (b, i, COL_MQ // MEM_WIDTH)),
            pl.BlockSpec((1, mem_len, MEM_WIDTH), lambda b, i: (b, 0, 0)),
            pl.BlockSpec((1, mem_len, MEM_WIDTH), lambda b, i: (b, 0, 1)),
        ],
        out_specs=pl.BlockSpec((1, tq, MEM_WIDTH), lambda b, i: (b, i, 0)),
        out_shape=jax.ShapeDtypeStruct((batch, seq, MEM_WIDTH), BF16),
        compiler_params=_params("parallel", "parallel"),
        name="mem_attn",
    )(proj3, mkv3, mkv3)


def _merge_kernel(yf_ref, ym_ref, yc_ref, wf_ref, wm_ref, wc_ref, g0_ref, g1_ref, g2_ref, o_ref,
                  wfb_ref, wmb_ref, wcb_ref):
    @pl.when(pl.program_id(1) == 0)
    def _():
        for src, dst in ((wf_ref, wfb_ref), (wm_ref, wmb_ref), (wc_ref, wcb_ref)):
            _cast_weight(dst, lambda sl, src=src: src[sl, :], src.shape[0])

    acc = g0_ref[...].astype(F32) * _dot(yf_ref[...], wfb_ref[...])
    acc = acc + g1_ref[...].astype(F32) * _dot(ym_ref[...], wmb_ref[...])
    acc = acc + g2_ref[...].astype(F32) * _dot(yc_ref[...], wcb_ref[...])
    o_ref[...] = acc.astype(BF16)


def _merge(y_fox, y_ml, y_mem, w_f, w_m, w_c, proj, bm=512, bn=512):
    m = y_fox.shape[0]
    kdim = y_fox.shape[1]
    y_spec = pl.BlockSpec((bm, kdim), lambda j, i: (i, 0))
    w_spec = pl.BlockSpec((kdim, bn), lambda j, i: (0, j))

    def gate_spec(branch):
        base = (COL_GATES + branch * D_MODEL) // bn
        return pl.BlockSpec((bm, bn), lambda j, i: (i, base + j))

    return pl.pallas_call(
        _merge_kernel,
        grid=(D_MODEL // bn, m // bm),
        in_specs=[y_spec, y_spec, y_spec, w_spec, w_spec, w_spec,
                  gate_spec(0), gate_spec(1), gate_spec(2)],
        out_specs=pl.BlockSpec((bm, bn), lambda j, i: (i, j)),
        out_shape=jax.ShapeDtypeStruct((m, D_MODEL), BF16),
        scratch_shapes=[pltpu.VMEM((kdim, bn), BF16)] * 3,
        compiler_params=_params("arbitrary", "arbitrary"),
        name="merge",
    )(y_fox, y_ml, y_mem, w_f, w_m, w_c, proj, proj, proj)


def _out_proj_kernel(a_ref, w_ref, x_ref, g_ref, h_ref, u_ref, wb_ref):
    @pl.when(pl.program_id(0) == 0)
    def _():
        _cast_weight(wb_ref, lambda sl: w_ref[sl, :], D_MODEL)

    h = x_ref[...] + _dot(a_ref[...], wb_ref[...])
    h_ref[...] = h
    u_ref[...] = _rms(h, g_ref[...]).astype(BF16)


def _out_proj(merged, w_out, x2, g_ffn, bm=256):
    m = merged.shape[0]
    return pl.pallas_call(
        _out_proj_kernel,
        grid=(m // bm,),
        in_specs=[
            pl.BlockSpec((bm, D_MODEL), lambda i: (i, 0)),
            pl.BlockSpec((D_MODEL, D_MODEL), lambda i: (0, 0), pipeline_mode=pl.Buffered(1)),
            pl.BlockSpec((bm, D_MODEL), lambda i: (i, 0)),
            pl.BlockSpec((1, D_MODEL), lambda i: (0, 0)),
        ],
        out_specs=[
            pl.BlockSpec((bm, D_MODEL), lambda i: (i, 0)),
            pl.BlockSpec((bm, D_MODEL), lambda i: (i, 0)),
        ],
        out_shape=[
            jax.ShapeDtypeStruct((m, D_MODEL), F32),
            jax.ShapeDtypeStruct((m, D_MODEL), BF16),
        ],
        scratch_shapes=[pltpu.VMEM((D_MODEL, D_MODEL), BF16)],
        compiler_params=_params("arbitrary"),
        name="out_proj",
    )(merged, w_out, x2, g_ffn)


def _ffn_in_kernel(u_ref, wg_ref, wu_ref, o_ref, wgb_ref, wub_ref):
    @pl.when(pl.program_id(1) == 0)
    def _():
        _cast_weight(wgb_ref, lambda sl: wg_ref[sl, :], D_MODEL)
        _cast_weight(wub_ref, lambda sl: wu_ref[sl, :], D_MODEL)

    u = u_ref[...]
    gate = _dot(u, wgb_ref[...])
    up = _dot(u, wub_ref[...])
    o_ref[...] = (gate * jax.nn.sigmoid(gate) * up).astype(BF16)


def _ffn_in(u, w, bm=1024, bn=512):
    m = u.shape[0]
    nb = D_FF // bn
    return pl.pallas_call(
        _ffn_in_kernel,
        grid=(nb, m // bm),
        in_specs=[
            pl.BlockSpec((bm, D_MODEL), lambda j, i: (i, 0)),
            pl.BlockSpec((D_MODEL, bn), lambda j, i: (0, j)),
            pl.BlockSpec((D_MODEL, bn), lambda j, i: (0, nb + j)),
        ],
        out_specs=pl.BlockSpec((bm, bn), lambda j, i: (i, j)),
        out_shape=jax.ShapeDtypeStruct((m, D_FF), BF16),
        scratch_shapes=[pltpu.VMEM((D_MODEL, bn), BF16)] * 2,
        compiler_params=_params("arbitrary", "arbitrary"),
        name="ffn_in",
    )(u, w, w)


def _ffn_out_kernel(a_ref, w_ref, r_ref, o_ref, wb_ref):
    @pl.when(pl.program_id(1) == 0)
    def _():
        _cast_weight(wb_ref, lambda sl: w_ref[sl, :], D_FF)

    o_ref[...] = r_ref[...] + _dot(a_ref[...], wb_ref[...])


def _ffn_out(act, w, resid, bm=512, bn=512):
    m = act.shape[0]
    return pl.pallas_call(
        _ffn_out_kernel,
        grid=(D_MODEL // bn, m // bm),
        in_specs=[
            pl.BlockSpec((bm, D_FF), lambda j, i: (i, 0)),
            pl.BlockSpec((D_FF, bn), lambda j, i: (0, j)),
            pl.BlockSpec((bm, bn), lambda j, i: (i, j)),
        ],
        out_specs=pl.BlockSpec((bm, bn), lambda j, i: (i, j)),
        out_shape=jax.ShapeDtypeStruct((m, D_MODEL), F32),
        scratch_shapes=[pltpu.VMEM((D_FF, bn), BF16)],
        compiler_params=_params("arbitrary", "arbitrary"),
        name="ffn_out",
    )(act, w, resid)


def _norm_kernel(h_ref, g_ref, o_ref):
    o_ref[...] = _rms(h_ref[...], g_ref[...])


def _final_norm(h, g, bm=512):
    m = h.shape[0]
    return pl.pallas_call(
        _norm_kernel,
        grid=(m // bm,),
        in_specs=[pl.BlockSpec((bm, D_MODEL), lambda i: (i, 0)),
                  pl.BlockSpec((1, D_MODEL), lambda i: (0, 0))],
        out_specs=pl.BlockSpec((bm, D_MODEL), lambda i: (i, 0)),
        out_shape=jax.ShapeDtypeStruct((m, D_MODEL), F32),
        compiler_params=_params("parallel"),
        name="final_norm",
    )(h, g)


def _layer(h2, mem2, batch, seq, norm_mix, w_in, b_in, conv_w, conv_b, mlstm_norm, norm_mem,
           w_mem_kv, w_br_fox, w_br_mlstm, w_br_mem, w_out, norm_ffn, w_ffn_in, w_ffn_out):
    b_big = jnp.concatenate([b_in[:OFF_FF], b_in[OFF_LQ:OFF_LI], b_in[OFF_LO:]])[None, :]
    b_small_t = jnp.concatenate([b_in[OFF_FF:OFF_LQ], b_in[OFF_LI:OFF_LO]])[:, None]

    w_in_t = w_in.T
    u, gst = _norm_in(h2, norm_mix[None, :], w_in_t, b_small_t)
    proj = _in_proj(u, w_in_t, b_big, conv_w, conv_b[None, :], seq)
    rows, cols = _gates(gst, batch, seq)
    proj3 = proj.reshape(batch, seq, PROJ_BIG)

    y_fox = _fox(proj3, cols)
    y_ml = _mlstm(proj3, mlstm_norm[None, :], rows, cols)
    mkv = _mem_kv(mem2, norm_mem[None, :], w_mem_kv)
    y_mem = _mem_attn(proj3, mkv.reshape(batch, -1, 2 * MEM_WIDTH))

    tokens = batch * seq
    merged = _merge(y_fox.reshape(tokens, FOX_WIDTH), y_ml.reshape(tokens, MLSTM_WIDTH),
                    y_mem.reshape(tokens, MEM_WIDTH), w_br_fox, w_br_mlstm, w_br_mem, proj)
    h2, u_ffn = _out_proj(merged, w_out, h2, norm_ffn[None, :])
    act = _ffn_in(u_ffn, w_ffn_in)
    return _ffn_out(act, w_ffn_out, h2)


def kernel(x, mem, norm_mix, w_in, b_in, conv_w, conv_b, mlstm_norm, norm_mem, w_mem_kv, w_br_fox,
           w_br_mlstm, w_br_mem, w_out, norm_ffn, w_ffn_in, w_ffn_out, norm_final):
    batch, seq, d = x.shape
    assert d == D_MODEL and seq % CHUNK == 0
    h2 = x.reshape(batch * seq, d)
    mem2 = mem.reshape(batch * mem.shape[1], d)
    for l in range(norm_mix.shape[0]):
        h2 = _layer(h2, mem2, batch, seq, norm_mix[l], w_in[l], b_in[l], conv_w[l], conv_b[l],
                    mlstm_norm[l], norm_mem[l], w_mem_kv[l], w_br_fox[l], w_br_mlstm[l],
                    w_br_mem[l], w_out[l], norm_ffn[l], w_ffn_in[l], w_ffn_out[l])
    return _final_norm(h2, norm_final[None, :]).reshape(batch, seq, d)
```

```python
import functools

import jax
import jax.numpy as jnp
from jax import lax
from jax.experimental import pallas as pl
from jax.experimental.pallas import tpu as pltpu

D_MODEL = 2048
FOX_HEADS = 8
FOX_HEAD_DIM = 128
FOX_WIDTH = FOX_HEADS * FOX_HEAD_DIM
MLSTM_HEADS = 4
MLSTM_HEAD_DIM = 256
MLSTM_WIDTH = MLSTM_HEADS * MLSTM_HEAD_DIM
MEM_HEADS = 4
MEM_HEAD_DIM = 256
MEM_WIDTH = MEM_HEADS * MEM_HEAD_DIM
N_BRANCH = 3
CONV_WIDTH = 4
CHUNK = 128
D_FF = 5632
EPS = 1e-6

OFF_FF = 3 * FOX_WIDTH
OFF_LQ = OFF_FF + FOX_HEADS
OFF_LI = OFF_LQ + 3 * MLSTM_WIDTH
OFF_LO = OFF_LI + 2 * MLSTM_HEADS
N_SMALL = FOX_HEADS + 2 * MLSTM_HEADS
PROJ_BIG = 3 * FOX_WIDTH + 3 * MLSTM_WIDTH + MLSTM_WIDTH + MEM_WIDTH + N_BRANCH * D_MODEL

COL_LQ = 3 * FOX_WIDTH
COL_LK = COL_LQ + MLSTM_WIDTH
COL_LV = COL_LK + MLSTM_WIDTH
COL_LO = COL_LV + MLSTM_WIDTH
COL_MQ = COL_LO + MLSTM_WIDTH
COL_GATES = COL_MQ + MEM_WIDTH

ROW_FOX = 0
ROW_MLI = FOX_HEADS
ROW_MLB = FOX_HEADS + MLSTM_HEADS

V7X_LANES = 128
V7X_VMEM_LIMIT = 56 * 1024 * 1024

F32 = jnp.float32
BF16 = jnp.bfloat16


def _params(*sem):
    return pltpu.CompilerParams(dimension_semantics=sem, vmem_limit_bytes=V7X_VMEM_LIMIT)


def _rms(x, g):
    return x * lax.rsqrt(jnp.mean(x * x, axis=-1, keepdims=True) + EPS) * g


def _dot(a, b):
    return jnp.dot(a, b, preferred_element_type=F32)


def _dot_nt(a, b):
    return lax.dot_general(a, b, (((1,), (1,)), ((), ())), preferred_element_type=F32)


def _dot_tn(a, b):
    return lax.dot_general(a, b, (((0,), (0,)), ((), ())), preferred_element_type=F32)


def _log_sigmoid(x):
    return jnp.minimum(x, 0.0) - jnp.log1p(jnp.exp(-jnp.abs(x)))


SUBLANES = 8


def _norm_in_kernel(x_ref, g_ref, wf_ref, wl_ref, bst_ref, u_ref, gst_ref, wt_ref):
    @pl.when(pl.program_id(0) == 0)
    def _():
        wt_ref[...] = jnp.concatenate([wf_ref[...], wl_ref[...]], axis=0).astype(BF16)

    u = _rms(x_ref[...], g_ref[...]).astype(BF16)
    u_ref[...] = u
    gst_ref[...] = _dot_nt(wt_ref[...], u) + bst_ref[...]


def _norm_in(x2, g, w_in_t, b_small_t, bm=512):
    m = x2.shape[0]
    assert FOX_HEADS == SUBLANES and 2 * MLSTM_HEADS == SUBLANES
    return pl.pallas_call(
        _norm_in_kernel,
        grid=(m // bm,),
        in_specs=[
            pl.BlockSpec((bm, D_MODEL), lambda i: (i, 0)),
            pl.BlockSpec((1, D_MODEL), lambda i: (0, 0)),
            pl.BlockSpec((SUBLANES, D_MODEL), lambda i: (OFF_FF // SUBLANES, 0)),
            pl.BlockSpec((SUBLANES, D_MODEL), lambda i: (OFF_LI // SUBLANES, 0)),
            pl.BlockSpec((N_SMALL, 1), lambda i: (0, 0)),
        ],
        out_specs=[
            pl.BlockSpec((bm, D_MODEL), lambda i: (i, 0)),
            pl.BlockSpec((N_SMALL, bm), lambda i: (0, i)),
        ],
        out_shape=[
            jax.ShapeDtypeStruct((m, D_MODEL), BF16),
            jax.ShapeDtypeStruct((N_SMALL, m), F32),
        ],
        scratch_shapes=[pltpu.VMEM((N_SMALL, D_MODEL), BF16)],
        compiler_params=_params("arbitrary"),
        name="norm_in",
    )(x2, g, w_in_t, w_in_t, b_small_t)


CAST_ROWS = 256
EPILOGUE_ROWS = 256


def _cast_weight(dst_ref, src_fn, rows):
    def chunk(c, carry):
        sl = pl.ds(pl.multiple_of(c * CAST_ROWS, CAST_ROWS), CAST_ROWS)
        dst_ref[sl, :] = src_fn(sl).astype(BF16)
        return carry

    lax.fori_loop(0, rows // CAST_ROWS, chunk, 0)


def _in_proj_kernel(u_ref, w_ref, wx_ref, b_ref, cw_ref, cb_ref, o_ref, wb_ref, xs_ref,
                    *, bn, tiles_per_seq):
    j = pl.program_id(0)
    first_lq = OFF_FF // bn
    first_lo = (OFF_LI - FOX_HEADS) // bn

    def cast_shifted(shift):
        def body():
            def chunk(c, carry):
                src = pl.ds(pl.multiple_of(c * CAST_ROWS + shift, SUBLANES), CAST_ROWS)
                dst = pl.ds(pl.multiple_of(c * CAST_ROWS, CAST_ROWS), CAST_ROWS)
                wb_ref[dst, :] = w_ref[src, :].astype(BF16)
                return carry

            lax.fori_loop(0, bn // CAST_ROWS - 1, chunk, 0)
            last = bn - CAST_ROWS
            tail = jnp.concatenate([w_ref[last + shift:bn, :], wx_ref[0:shift, :]], axis=0)
            wb_ref[last:bn, :] = tail.astype(BF16)
        return body

    @pl.when(pl.program_id(1) == 0)
    def _():
        @pl.when(j < first_lq)
        def _():
            _cast_weight(wb_ref, lambda sl: w_ref[sl, :], bn)

        pl.when(jnp.logical_and(j >= first_lq, j < first_lo))(cast_shifted(FOX_HEADS))
        pl.when(j >= first_lo)(cast_shifted(N_SMALL))

    i = pl.program_id(1)
    bm = u_ref.shape[0]
    t_lq, t_lk, t_lo, t_mq = (c // bn for c in (COL_LQ, COL_LK, COL_LO, COL_MQ))
    is_conv = jnp.logical_or(j == t_lq, j == t_lk)
    is_gate = jnp.logical_or(j == t_lo, j > t_mq)

    row_tiles = [slice(r0, r0 + EPILOGUE_ROWS) for r0 in range(0, bm, EPILOGUE_ROWS)]

    def raw(rs):
        return _dot_nt(u_ref[rs, :], wb_ref[...]) + b_ref[...]

    @pl.when(jnp.logical_not(jnp.logical_or(is_conv, is_gate)))
    def _():
        mult = jnp.where(j == 0, FOX_Q_SCALE, jnp.where(j == t_mq, MEM_HEAD_DIM ** -0.5, 1.0))
        for rs in row_tiles:
            o_ref[rs, :] = (raw(rs) * mult).astype(BF16)

    @pl.when(is_gate)
    def _():
        for rs in row_tiles:
            o_ref[rs, :] = jax.nn.sigmoid(raw(rs)).astype(BF16)

    @pl.when(is_conv)
    def _():
        cw = cw_ref[...]
        mult = jnp.where(j == t_lk, MLSTM_HEAD_DIM ** -0.5, 1.0)
        @pl.when(i % tiles_per_seq == 0)
        def _():
            xs_ref[0, 0:SUBLANES, :] = jnp.zeros((SUBLANES, bn), F32)

        for n, rs in enumerate(row_tiles):
            cur, nxt = n % 2, (n + 1) % 2
            x = raw(rs)
            xs_ref[cur, SUBLANES:SUBLANES + EPILOGUE_ROWS, :] = x
            xs_ref[nxt, 0:SUBLANES, :] = x[EPILOGUE_ROWS - SUBLANES:EPILOGUE_ROWS, :]
            y = cb_ref[...] + cw[CONV_WIDTH - 1:CONV_WIDTH, :] * x
            for t in range(1, CONV_WIDTH):
                y = y + (cw[CONV_WIDTH - 1 - t:CONV_WIDTH - t, :]
                         * xs_ref[cur, SUBLANES - t:SUBLANES - t + EPILOGUE_ROWS, :])
            o_ref[rs, :] = (y * jax.nn.sigmoid(y) * mult).astype(BF16)


def _in_proj(u, w_in_t, b_big, conv_w, conv_b, seq, bm=2048, bn=FOX_WIDTH):
    m = u.shape[0]
    assert bn == FOX_WIDTH and OFF_FF % bn == 0 and (OFF_LI - FOX_HEADS) % bn == 0
    assert seq % bm == 0 and MLSTM_WIDTH == bn and CONV_WIDTH - 1 <= SUBLANES
    extra_per_tile = bn // N_SMALL
    t_lq = COL_LQ // bn
    return pl.pallas_call(
        functools.partial(_in_proj_kernel, bn=bn, tiles_per_seq=seq // bm),
        grid=(PROJ_BIG // bn, m // bm),
        in_specs=[
            pl.BlockSpec((bm, D_MODEL), lambda j, i: (i, 0)),
            pl.BlockSpec((bn, D_MODEL), lambda j, i: (j, 0)),
            pl.BlockSpec((N_SMALL, D_MODEL), lambda j, i: ((j + 1) * extra_per_tile, 0)),
            pl.BlockSpec((1, bn), lambda j, i: (0, j)),
            pl.BlockSpec((CONV_WIDTH, bn), lambda j, i: (0, jnp.clip(j - t_lq, 0, 1))),
            pl.BlockSpec((1, bn), lambda j, i: (0, jnp.clip(j - t_lq, 0, 1))),
        ],
        out_specs=pl.BlockSpec((bm, bn), lambda j, i: (i, j)),
        out_shape=jax.ShapeDtypeStruct((m, PROJ_BIG), BF16),
        scratch_shapes=[pltpu.VMEM((bn, D_MODEL), BF16),
                        pltpu.VMEM((2, SUBLANES + EPILOGUE_ROWS, bn), F32)],
        compiler_params=_params("arbitrary", "arbitrary"),
        name="in_proj",
    )(u, w_in_t, w_in_t, b_big, conv_w, conv_b)


def _gates_kernel(gst_ref, rows_ref, cols_ref, *, seq):
    g = gst_ref[...]
    row = lax.broadcasted_iota(jnp.int32, g.shape, 0)
    lane = lax.broadcasted_iota(jnp.int32, g.shape, 1)
    in_chunk = lane % CHUNK
    is_input_gate = jnp.logical_and(row >= ROW_MLI, row < ROW_MLB)
    local = jnp.where(is_input_gate, g, _log_sigmoid(g))
    d = 1
    while d < CHUNK:
        local = local + jnp.where(in_chunk >= d, pltpu.roll(local, d, axis=1), 0.0)
        d *= 2
    carry = pltpu.roll(jnp.where(in_chunk == CHUNK - 1, local, 0.0), 1, axis=1)
    carry = jnp.where(lane == 0, 0.0, carry)
    d = 1
    while d < CHUNK:
        carry = carry + jnp.where(in_chunk >= d, pltpu.roll(carry, d, axis=1), 0.0)
        d *= 2
    while d < seq:
        carry = carry + jnp.concatenate([jnp.zeros((N_SMALL, d), F32), carry[:, :seq - d]], axis=1)
        d *= 2
    out = jnp.where(row < ROW_MLI, local + carry, jnp.where(is_input_gate, g, local))
    rows_ref[0] = out
    padded = jnp.concatenate([out, jnp.zeros((V7X_LANES - N_SMALL, seq), F32)], axis=0)
    cols_ref[0] = padded.T


def _gates(gst, batch, seq):
    return pl.pallas_call(
        functools.partial(_gates_kernel, seq=seq),
        grid=(batch,),
        in_specs=[pl.BlockSpec((N_SMALL, seq), lambda b: (0, b))],
        out_specs=[
            pl.BlockSpec((1, N_SMALL, seq), lambda b: (b, 0, 0)),
            pl.BlockSpec((1, seq, V7X_LANES), lambda b: (b, 0, 0)),
        ],
        out_shape=[
            jax.ShapeDtypeStruct((batch, N_SMALL, seq), F32),
            jax.ShapeDtypeStruct((batch, seq, V7X_LANES), F32),
        ],
        compiler_params=_params("parallel"),
        name="gates",
    )(gst)


FOX_TQ = 1024
FOX_SUB = 128
LOG2E = 1.4426950408889634
FOX_Q_SCALE = FOX_HEAD_DIM ** -0.5 * LOG2E


def _split3(f):
    hi = f.astype(BF16).astype(F32)
    r = f - hi
    mid = r.astype(BF16).astype(F32)
    return hi, mid, r - mid


def _fox_kernel(q_ref, k_ref, v_ref, fcol_ref, o_ref, kaug_ref, vaug_ref, q2_ref, m_ref, acc_ref,
                s_ref, p_ref, alpha_ref, *, tq, sub, seq):
    h = pl.program_id(1)
    qi = pl.program_id(2)
    dh = FOX_HEAD_DIM

    def f_column(rows):
        lane = lax.broadcasted_iota(jnp.int32, (rows.shape[0], V7X_LANES), 1)
        return jnp.sum(jnp.where(lane == h, rows, 0.0), axis=-1, keepdims=True) * LOG2E

    tk = tq // 2

    @pl.when(qi == 0)
    def _():
        for c in range(seq // tk):
            sl = pl.ds(c * tk, tk)
            hi, mid, lo = _split3(f_column(fcol_ref[0, sl, :]))
            lane = lax.broadcasted_iota(jnp.int32, (tk, V7X_LANES), 1)
            aug = jnp.where(lane < 3, 1.0, jnp.where(lane == 3, -hi, jnp.where(lane == 4, -mid,
                            jnp.where(lane == 5, -lo, 0.0))))
            k2 = jnp.concatenate([k_ref[0, sl, :].astype(F32), aug], axis=1)
            kaug_ref[c] = k2.T.astype(BF16)
            vaug_ref[sl, 0:dh] = v_ref[0, sl, :]
            vaug_ref[sl, dh:2 * dh] = jnp.where(lane == 0, 1.0, 0.0).astype(BF16)

    q_start = pl.multiple_of(qi * tq, tq)
    hi, mid, lo = _split3(f_column(fcol_ref[0, pl.ds(q_start, tq), :]))
    lane = lax.broadcasted_iota(jnp.int32, (tq, V7X_LANES), 1)
    qaug = jnp.where(lane == 0, hi, jnp.where(lane == 1, mid, jnp.where(lane == 2, lo,
                     jnp.where(lane < 6, 1.0, 0.0))))
    q2_ref[:, 0:dh] = q_ref[0]
    q2_ref[:, dh:2 * dh] = qaug.astype(BF16)

    m_ref[...] = jnp.full_like(m_ref, -jnp.inf)
    acc_ref[...] = jnp.zeros_like(acc_ref)

    all_subs = [slice(r0, r0 + sub) for r0 in range(0, tq, sub)]
    low_subs = [rs for rs in all_subs if rs.start >= tk]

    def logits(t, slot, subs):
        kt = kaug_ref[t]
        for rs in subs:
            s_ref[slot, rs, :] = _dot(q2_ref[rs, :], kt)

    def softmax(slot, subs, col0=None):
        for rs in subs:
            s = s_ref[slot, rs, :]
            if col0 is not None and rs.start < col0 + tk:
                r = lax.broadcasted_iota(jnp.int32, (sub, tk), 0) + rs.start
                c = lax.broadcasted_iota(jnp.int32, (sub, tk), 1) + col0
                s = jnp.where(c <= r, s, -jnp.inf)
            m_prev = m_ref[rs, :]
            m_new = jnp.maximum(m_prev, jnp.broadcast_to(jnp.max(s, axis=-1, keepdims=True), m_prev.shape))
            alpha_ref[slot, rs, :] = jnp.exp2(m_prev - m_new)
            for c0 in range(0, tk, V7X_LANES):
                cs = slice(c0, c0 + V7X_LANES)
                p_ref[slot, rs, cs] = jnp.exp2(s[:, cs] - m_new).astype(BF16)
            m_ref[rs, :] = m_new

    def values(t, slot, subs):
        start = pl.multiple_of(t * tk, tk)
        va = vaug_ref[pl.ds(start, tk), :]
        for rs in subs:
            pv = _dot(p_ref[slot, rs, :], va)
            alpha = alpha_ref[slot, rs, :]
            for c0 in range(0, 2 * dh, V7X_LANES):
                cs = slice(c0, c0 + V7X_LANES)
                acc_ref[rs, cs] = alpha * acc_ref[rs, cs] + pv[:, cs]

    p_ref[1] = jnp.zeros((tq, tk), BF16)
    alpha_ref[1] = jnp.ones((tq, V7X_LANES), F32)
    logits(0, 0, all_subs)

    def pair(u, carry):
        t = 2 * u
        values(jnp.maximum(t - 1, 0), 1, all_subs)
        softmax(0, all_subs)
        logits(t + 1, 1, all_subs)
        values(t, 0, all_subs)
        softmax(1, all_subs)
        logits(t + 2, 0, all_subs)
        return carry

    lax.fori_loop(0, qi, pair, 0)
    t = 2 * qi
    values(jnp.maximum(t - 1, 0), 1, all_subs)
    softmax(0, all_subs, col0=0)
    logits(t + 1, 1, low_subs)
    values(t, 0, all_subs)
    softmax(1, low_subs, col0=tk)
    values(t + 1, 1, low_subs)
    o_ref[0] = (acc_ref[:, 0:dh] / acc_ref[:, dh:dh + 1]).astype(BF16)


def _fox(proj3, cols, tq=FOX_TQ, sub=FOX_SUB):
    batch, seq, _ = proj3.shape
    dh = FOX_HEAD_DIM
    tk = tq // 2
    return pl.pallas_call(
        functools.partial(_fox_kernel, tq=tq, sub=sub, seq=seq),
        grid=(batch, FOX_HEADS, seq // tq),
        in_specs=[
            pl.BlockSpec((1, tq, dh), lambda b, h, i: (b, i, h)),
            pl.BlockSpec((1, seq, dh), lambda b, h, i: (b, 0, FOX_HEADS + h)),
            pl.BlockSpec((1, seq, dh), lambda b, h, i: (b, 0, 2 * FOX_HEADS + h)),
            pl.BlockSpec((1, seq, V7X_LANES), lambda b, h, i: (b, 0, 0)),
        ],
        out_specs=pl.BlockSpec((1, tq, dh), lambda b, h, i: (b, i, h)),
        out_shape=jax.ShapeDtypeStruct((batch, seq, FOX_WIDTH), BF16),
        scratch_shapes=[pltpu.VMEM((seq // tk, 2 * dh, tk), BF16), pltpu.VMEM((seq, 2 * dh), BF16),
                        pltpu.VMEM((tq, 2 * dh), BF16), pltpu.VMEM((tq, V7X_LANES), F32),
                        pltpu.VMEM((tq, 2 * dh), F32), pltpu.VMEM((2, tq, tk), F32),
                        pltpu.VMEM((2, tq, tk), BF16), pltpu.VMEM((2, tq, V7X_LANES), F32)],
        compiler_params=_params("parallel", "parallel", "arbitrary"),
        name="fox_attn",
    )(proj3, proj3, proj3, cols)


def _mlstm_kernel(q_ref, k_ref, v_ref, o_ref, gn_ref, rows_ref, cols_ref, y_ref, c_ref, n_ref, m_ref,
                  *, batch):
    j = pl.program_id(0)
    L = CHUNK
    dh = MLSTM_HEAD_DIM

    @pl.when(j == 0)
    def _():
        c_ref[...] = jnp.zeros_like(c_ref)
        n_ref[...] = jnp.zeros_like(n_ref)
        m_ref[...] = jnp.zeros_like(m_ref)

    lanes = V7X_LANES
    assert L == lanes and dh % lanes == 0
    halves = [slice(c0, c0 + lanes) for c0 in range(0, dh, lanes)]
    r = lax.broadcasted_iota(jnp.int32, (L, L), 0)
    c = lax.broadcasted_iota(jnp.int32, (L, L), 1)
    causal = c <= r
    chains = [(b, h) for b in range(batch) for h in range(MLSTM_HEADS)]

    def rep(x):
        return jnp.broadcast_to(x, (L, lanes))

    sel_r = lax.broadcasted_iota(jnp.int32, (3 * lanes, 2 * lanes), 0) % lanes
    sel_c = lax.broadcasted_iota(jnp.int32, (3 * lanes, 2 * lanes), 1)
    pieces = []
    for b in range(batch):
        hi, mid, lo = _split3(cols_ref[b])
        pieces.append(jnp.concatenate([hi, mid, lo], axis=1).astype(BF16))

    st1 = []
    for b, h in chains:
        hs = slice(h * dh, (h + 1) * dh)
        st = b * MLSTM_HEADS + h
        sel = sel_r == jnp.where(sel_c < lanes, ROW_MLB + h, ROW_MLI + h)
        bi = _dot(pieces[b], jnp.where(sel, 1.0, 0.0).astype(BF16))
        bcol, icol = bi[:, 0:lanes], bi[:, lanes:2 * lanes]
        qb = q_ref[b, :, hs]
        kb = k_ref[b, :, hs]
        brow = rows_ref[b, ROW_MLB + h:ROW_MLB + h + 1, :]
        irow = rows_ref[b, ROW_MLI + h:ROW_MLI + h + 1, :]
        m_prev = m_ref[st]
        c_prev = c_ref[st]
        qk_raw = _dot_nt(qb, kb)
        qc = _dot(qb, c_prev.astype(BF16))

        dlog = jnp.where(causal, bcol - brow + irow, -jnp.inf)
        inter = bcol + m_prev
        m_t = jnp.maximum(inter, rep(jnp.max(dlog, axis=-1, keepdims=True)))
        w = jnp.exp(dlog - m_t)
        a = jnp.exp(inter - m_t)
        b_end = brow[:, L - 1:L]
        m_new = jnp.maximum(b_end + m_prev, jnp.max(b_end - brow + irow, axis=-1, keepdims=True))
        decay = jnp.exp(b_end + m_prev - m_new)
        ws = jnp.exp(b_end - bcol + icol - m_new)
        qn = rep(jnp.sum(qb.astype(F32) * n_ref[st], axis=-1, keepdims=True))
        st1.append((hs, st, kb, c_prev, qk_raw, qc, m_t, w, a, m_new, decay, ws, qn))

    st2 = []
    for (b, h), (hs, st, kb, c_prev, qk_raw, qc, m_t, w, a, m_new, decay, ws, qn) in zip(chains, st1):
        v = v_ref[b, :, hs]
        qk = qk_raw * w
        k = kb.astype(F32)
        kw = jnp.concatenate([k[:, cs] * ws for cs in halves], axis=1)
        pv = _dot(qk.astype(BF16), v)
        upd = _dot_tn(kw.astype(BF16), v)
        st2.append((qk, kw, pv, upd))

    for (b, h), s1, (qk, kw, pv, upd) in zip(chains, st1, st2):
        hs, st, kb, c_prev, qk_raw, qc, m_t, w, a, m_new, decay, ws, qn = s1
        den = a * qn + rep(jnp.sum(qk, axis=-1, keepdims=True))
        inv = 1.0 / jnp.maximum(jnp.abs(den), jnp.exp(-m_t))
        h_parts = [(a * qc[:, cs] + pv[:, cs]) * inv for cs in halves]
        sq = h_parts[0] * h_parts[0]
        for hp in h_parts[1:]:
            sq = sq + hp * hp
        rms = lax.rsqrt(rep(jnp.sum(sq, axis=-1, keepdims=True)) * (1.0 / dh) + EPS)
        for cs, hp in zip(halves, h_parts):
            oc = slice(hs.start + cs.start, hs.start + cs.stop)
            y = hp * rms * gn_ref[:, oc] * o_ref[b, :, oc].astype(F32)
            y_ref[b, :, oc] = y.astype(BF16)
        c_ref[st] = decay * c_prev + upd
        n_ref[st] = decay * n_ref[st] + jnp.sum(kw, axis=0, keepdims=True)
        m_ref[st] = m_new


def _mlstm(proj3, gn, rows, cols):
    batch, seq, _ = proj3.shape
    dh = MLSTM_HEAD_DIM
    width = MLSTM_WIDTH
    nc = seq // CHUNK
    chains = batch * MLSTM_HEADS

    def col(base):
        return lambda j: (0, j, base // width)

    return pl.pallas_call(
        functools.partial(_mlstm_kernel, batch=batch),
        grid=(nc,),
        in_specs=[
            pl.BlockSpec((batch, CHUNK, width), col(COL_LQ)),
            pl.BlockSpec((batch, CHUNK, width), col(COL_LK)),
            pl.BlockSpec((batch, CHUNK, width), col(COL_LV)),
            pl.BlockSpec((batch, CHUNK, width), col(COL_LO)),
            pl.BlockSpec((1, width), lambda j: (0, 0)),
            pl.BlockSpec((batch, N_SMALL, CHUNK), lambda j: (0, 0, j)),
            pl.BlockSpec((batch, CHUNK, V7X_LANES), lambda j: (0, j, 0)),
        ],
        out_specs=pl.BlockSpec((batch, CHUNK, width), lambda j: (0, j, 0)),
        out_shape=jax.ShapeDtypeStruct((batch, seq, width), BF16),
        scratch_shapes=[pltpu.VMEM((chains, dh, dh), F32), pltpu.VMEM((chains, 1, dh), F32),
                        pltpu.VMEM((chains, 1, 1), F32)],
        compiler_params=_params("arbitrary"),
        name="mlstm",
    )(proj3, proj3, proj3, proj3, gn, rows, cols)


def _mem_kv_kernel(mem_ref, g_ref, w_ref, o_ref):
    u = _rms(mem_ref[...], g_ref[...]).astype(BF16)
    o_ref[...] = _dot(u, w_ref[...].astype(BF16)).astype(BF16)


def _mem_kv(mem2, g, w, bn=512):
    m = mem2.shape[0]
    n = w.shape[1]
    return pl.pallas_call(
        _mem_kv_kernel,
        grid=(n // bn,),
        in_specs=[
            pl.BlockSpec((m, D_MODEL), lambda j: (0, 0)),
            pl.BlockSpec((1, D_MODEL), lambda j: (0, 0)),
            pl.BlockSpec((D_MODEL, bn), lambda j: (0, j)),
        ],
        out_specs=pl.BlockSpec((m, bn), lambda j: (0, j)),
        out_shape=jax.ShapeDtypeStruct((m, n), BF16),
        compiler_params=_params("parallel"),
        name="mem_kv",
    )(mem2, g, w)


def _mem_attn_kernel(q_ref, k_ref, v_ref, o_ref):
    dh = MEM_HEAD_DIM
    for hh in range(MEM_HEADS):
        sl = slice(hh * dh, (hh + 1) * dh)
        s = _dot_nt(q_ref[0, :, sl], k_ref[0, :, sl])
        p = jnp.exp(s - jnp.max(s, axis=-1, keepdims=True))
        l = jnp.sum(p, axis=-1, keepdims=True)
        o_ref[0, :, sl] = (_dot(p.astype(BF16), v_ref[0, :, sl]) / l).astype(BF16)


def _mem_attn(proj3, mkv3, tq=512):
    batch, seq, _ = proj3.shape
    mem_len = mkv3.shape[1]
    return pl.pallas_call(
        _mem_attn_kernel,
        grid=(batch, seq // tq),
        in_specs=[
            pl.BlockSpec((1, tq, MEM_WIDTH), lambda b, i: (b, i, COL_MQ // MEM_WIDTH)),
            pl.BlockSpec((1, mem_len, MEM_WIDTH), lambda b, i: (b, 0, 0)),
            pl.BlockSpec((1, mem_len, MEM_WIDTH), lambda b, i: (b, 0, 1)),
        ],
        out_specs=pl.BlockSpec((1, tq, MEM_WIDTH), lambda b, i: (b, i, 0)),
        out_shape=jax.ShapeDtypeStruct((batch, seq, MEM_WIDTH), BF16),
        compiler_params=_params("parallel", "parallel"),
        name="mem_attn",
    )(proj3, mkv3, mkv3)


def _merge_kernel(yf_ref, ym_ref, yc_ref, wf_ref, wm_ref, wc_ref, g0_ref, g1_ref, g2_ref, o_ref,
                  wfb_ref, wmb_ref, wcb_ref):
    @pl.when(pl.program_id(1) == 0)
    def _():
        for src, dst in ((wf_ref, wfb_ref), (wm_ref, wmb_ref), (wc_ref, wcb_ref)):
            _cast_weight(dst, lambda sl, src=src: src[sl, :], src.shape[0])

    acc = g0_ref[...].astype(F32) * _dot(yf_ref[...], wfb_ref[...])
    acc = acc + g1_ref[...].astype(F32) * _dot(ym_ref[...], wmb_ref[...])
    acc = acc + g2_ref[...].astype(F32) * _dot(yc_ref[...], wcb_ref[...])
    o_ref[...] = acc.astype(BF16)


def _merge(y_fox, y_ml, y_mem, w_f, w_m, w_c, proj, bm=512, bn=1024):
    m = y_fox.shape[0]
    kdim = y_fox.shape[1]
    y_spec = pl.BlockSpec((bm, kdim), lambda j, i: (i, 0))
    w_spec = pl.BlockSpec((kdim, bn), lambda j, i: (0, j))

    def gate_spec(branch):
        base = (COL_GATES + branch * D_MODEL) // bn
        return pl.BlockSpec((bm, bn), lambda j, i: (i, base + j))

    return pl.pallas_call(
        _merge_kernel,
        grid=(D_MODEL // bn, m // bm),
        in_specs=[y_spec, y_spec, y_spec, w_spec, w_spec, w_spec,
                  gate_spec(0), gate_spec(1), gate_spec(2)],
        out_specs=pl.BlockSpec((bm, bn), lambda j, i: (i, j)),
        out_shape=jax.ShapeDtypeStruct((m, D_MODEL), BF16),
        scratch_shapes=[pltpu.VMEM((kdim, bn), BF16)] * 3,
        compiler_params=_params("arbitrary", "arbitrary"),
        name="merge",
    )(y_fox, y_ml, y_mem, w_f, w_m, w_c, proj, proj, proj)


def _out_proj_kernel(a_ref, w_ref, x_ref, g_ref, h_ref, u_ref, wb_ref):
    @pl.when(pl.program_id(0) == 0)
    def _():
        _cast_weight(wb_ref, lambda sl: w_ref[sl, :], D_MODEL)

    h = x_ref[...] + _dot(a_ref[...], wb_ref[...])
    h_ref[...] = h
    u_ref[...] = _rms(h, g_ref[...]).astype(BF16)


def _out_proj(merged, w_out, x2, g_ffn, bm=512):
    m = merged.shape[0]
    return pl.pallas_call(
        _out_proj_kernel,
        grid=(m // bm,),
        in_specs=[
            pl.BlockSpec((bm, D_MODEL), lambda i: (i, 0)),
            pl.BlockSpec((D_MODEL, D_MODEL), lambda i: (0, 0), pipeline_mode=pl.Buffered(1)),
            pl.BlockSpec((bm, D_MODEL), lambda i: (i, 0)),
            pl.BlockSpec((1, D_MODEL), lambda i: (0, 0)),
        ],
        out_specs=[
            pl.BlockSpec((bm, D_MODEL), lambda i: (i, 0)),
            pl.BlockSpec((bm, D_MODEL), lambda i: (i, 0)),
        ],
        out_shape=[
            jax.ShapeDtypeStruct((m, D_MODEL), F32),
            jax.ShapeDtypeStruct((m, D_MODEL), BF16),
        ],
        scratch_shapes=[pltpu.VMEM((D_MODEL, D_MODEL), BF16)],
        compiler_params=_params("arbitrary"),
        name="out_proj",
    )(merged, w_out, x2, g_ffn)


def _ffn_in_kernel(u_ref, wg_ref, wu_ref, o_ref, wgb_ref, wub_ref):
    @pl.when(pl.program_id(1) == 0)
    def _():
        _cast_weight(wgb_ref, lambda sl: wg_ref[sl, :], D_MODEL)
        _cast_weight(wub_ref, lambda sl: wu_ref[sl, :], D_MODEL)

    u = u_ref[...]
    gate = _dot(u, wgb_ref[...])
    up = _dot(u, wub_ref[...])
    o_ref[...] = (gate * jax.nn.sigmoid(gate) * up).astype(BF16)


def _ffn_in(u, w, bm=2048, bn=512):
    m = u.shape[0]
    nb = D_FF // bn
    return pl.pallas_call(
        _ffn_in_kernel,
        grid=(nb, m // bm),
        in_specs=[
            pl.BlockSpec((bm, D_MODEL), lambda j, i: (i, 0)),
            pl.BlockSpec((D_MODEL, bn), lambda j, i: (0, j)),
            pl.BlockSpec((D_MODEL, bn), lambda j, i: (0, nb + j)),
        ],
        out_specs=pl.BlockSpec((bm, bn), lambda j, i: (i, j)),
        out_shape=jax.ShapeDtypeStruct((m, D_FF), BF16),
        scratch_shapes=[pltpu.VMEM((D_MODEL, bn), BF16)] * 2,
        compiler_params=_params("arbitrary", "arbitrary"),
        name="ffn_in",
    )(u, w, w)


def _ffn_out_kernel(a_ref, w_ref, r_ref, o_ref, wb_ref):
    @pl.when(pl.program_id(1) == 0)
    def _():
        _cast_weight(wb_ref, lambda sl: w_ref[sl, :], D_FF)

    o_ref[...] = r_ref[...] + _dot(a_ref[...], wb_ref[...])


def _ffn_out(act, w, resid, bm=512, bn=512):
    m = act.shape[0]
    return pl.pallas_call(
        _ffn_out_kernel,
        grid=(D_MODEL // bn, m // bm),
        in_specs=[
            pl.BlockSpec((bm, D_FF), lambda j, i: (i, 0)),
            pl.BlockSpec((D_FF, bn), lambda j, i: (0, j)),
            pl.BlockSpec((bm, bn), lambda j, i: (i, j)),
        ],
        out_specs=pl.BlockSpec((bm, bn), lambda j, i: (i, j)),
        out_shape=jax.ShapeDtypeStruct((m, D_MODEL), F32),
        scratch_shapes=[pltpu.VMEM((D_FF, bn), BF16)],
        compiler_params=_params("arbitrary", "arbitrary"),
        name="ffn_out",
    )(act, w, resid)


def _norm_kernel(h_ref, g_ref, o_ref):
    o_ref[...] = _rms(h_ref[...], g_ref[...])


def _final_norm(h, g, bm=512):
    m = h.shape[0]
    return pl.pallas_call(
        _norm_kernel,
        grid=(m // bm,),
        in_specs=[pl.BlockSpec((bm, D_MODEL), lambda i: (i, 0)),
                  pl.BlockSpec((1, D_MODEL), lambda i: (0, 0))],
        out_specs=pl.BlockSpec((bm, D_MODEL), lambda i: (i, 0)),
        out_shape=jax.ShapeDtypeStruct((m, D_MODEL), F32),
        compiler_params=_params("parallel"),
        name="final_norm",
    )(h, g)


def _layer(h2, mem2, batch, seq, norm_mix, w_in, b_in, conv_w, conv_b, mlstm_norm, norm_mem,
           w_mem_kv, w_br_fox, w_br_mlstm, w_br_mem, w_out, norm_ffn, w_ffn_in, w_ffn_out):
    b_big = jnp.concatenate([b_in[:OFF_FF], b_in[OFF_LQ:OFF_LI], b_in[OFF_LO:]])[None, :]
    b_small_t = jnp.concatenate([b_in[OFF_FF:OFF_LQ], b_in[OFF_LI:OFF_LO]])[:, None]

    w_in_t = w_in.T
    u, gst = _norm_in(h2, norm_mix[None, :], w_in_t, b_small_t)
    proj = _in_proj(u, w_in_t, b_big, conv_w, conv_b[None, :], seq)
    rows, cols = _gates(gst, batch, seq)
    proj3 = proj.reshape(batch, seq, PROJ_BIG)

    y_fox = _fox(proj3, cols)
    y_ml = _mlstm(proj3, mlstm_norm[None, :], rows, cols)
    mkv = _mem_kv(mem2, norm_mem[None, :], w_mem_kv)
    y_mem = _mem_attn(proj3, mkv.reshape(batch, -1, 2 * MEM_WIDTH))

    tokens = batch * seq
    merged = _merge(y_fox.reshape(tokens, FOX_WIDTH), y_ml.reshape(tokens, MLSTM_WIDTH),
                    y_mem.reshape(tokens, MEM_WIDTH), w_br_fox, w_br_mlstm, w_br_mem, proj)
    h2, u_ffn = _out_proj(merged, w_out, h2, norm_ffn[None, :])
    act = _ffn_in(u_ffn, w_ffn_in)
    return _ffn_out(act, w_ffn_out, h2)


def kernel(x, mem, norm_mix, w_in, b_in, conv_w, conv_b, mlstm_norm, norm_mem, w_mem_kv, w_br_fox,
           w_br_mlstm, w_br_mem, w_out, norm_ffn, w_ffn_in, w_ffn_out, norm_final):
    batch, seq, d = x.shape
    assert d == D_MODEL and seq % CHUNK == 0
    h2 = x.reshape(batch * seq, d)
    mem2 = mem.reshape(batch * mem.shape[1], d)
    for l in range(norm_mix.shape[0]):
        h2 = _layer(h2, mem2, batch, seq, norm_mix[l], w_in[l], b_in[l], conv_w[l], conv_b[l],
                    mlstm_norm[l], norm_mem[l], w_mem_kv[l], w_br_fox[l], w_br_mlstm[l],
                    w_br_mem[l], w_out[l], norm_ffn[l], w_ffn_in[l], w_ffn_out[l])
    return _final_norm(h2, norm_final[None, :]).reshape(batch, seq, d)
```

```python
import functools

import jax
import jax.numpy as jnp
from jax import lax
from jax.experimental import pallas as pl
from jax.experimental.pallas import tpu as pltpu

D_MODEL = 2048
FOX_HEADS = 8
FOX_HEAD_DIM = 128
FOX_WIDTH = FOX_HEADS * FOX_HEAD_DIM
MLSTM_HEADS = 4
MLSTM_HEAD_DIM = 256
MLSTM_WIDTH = MLSTM_HEADS * MLSTM_HEAD_DIM
MEM_HEADS = 4
MEM_HEAD_DIM = 256
MEM_WIDTH = MEM_HEADS * MEM_HEAD_DIM
N_BRANCH = 3
CONV_WIDTH = 4
CHUNK = 128
D_FF = 5632
EPS = 1e-6

OFF_FF = 3 * FOX_WIDTH
OFF_LQ = OFF_FF + FOX_HEADS
OFF_LI = OFF_LQ + 3 * MLSTM_WIDTH
OFF_LO = OFF_LI + 2 * MLSTM_HEADS
N_SMALL = FOX_HEADS + 2 * MLSTM_HEADS
PROJ_BIG = 3 * FOX_WIDTH + 3 * MLSTM_WIDTH + MLSTM_WIDTH + MEM_WIDTH + N_BRANCH * D_MODEL

COL_LQ = 3 * FOX_WIDTH
COL_LK = COL_LQ + MLSTM_WIDTH
COL_LV = COL_LK + MLSTM_WIDTH
COL_LO = COL_LV + MLSTM_WIDTH
COL_MQ = COL_LO + MLSTM_WIDTH
COL_GATES = COL_MQ + MEM_WIDTH

ROW_FOX = 0
ROW_MLI = FOX_HEADS
ROW_MLB = FOX_HEADS + MLSTM_HEADS

V7X_LANES = 128
V7X_VMEM_LIMIT = 56 * 1024 * 1024

F32 = jnp.float32
BF16 = jnp.bfloat16


def _params(*sem):
    return pltpu.CompilerParams(dimension_semantics=sem, vmem_limit_bytes=V7X_VMEM_LIMIT)


def _rms(x, g):
    return x * lax.rsqrt(jnp.mean(x * x, axis=-1, keepdims=True) + EPS) * g


def _dot(a, b):
    return jnp.dot(a, b, preferred_element_type=F32)


def _dot_nt(a, b):
    return lax.dot_general(a, b, (((1,), (1,)), ((), ())), preferred_element_type=F32)


def _dot_tn(a, b):
    return lax.dot_general(a, b, (((0,), (0,)), ((), ())), preferred_element_type=F32)


def _log_sigmoid(x):
    return jnp.minimum(x, 0.0) - jnp.log1p(jnp.exp(-jnp.abs(x)))


SUBLANES = 8


def _norm_in_kernel(x_ref, g_ref, wf_ref, wl_ref, bst_ref, u_ref, gst_ref, wt_ref):
    @pl.when(pl.program_id(0) == 0)
    def _():
        wt_ref[...] = jnp.concatenate([wf_ref[...], wl_ref[...]], axis=0).astype(BF16)

    u = _rms(x_ref[...], g_ref[...]).astype(BF16)
    u_ref[...] = u
    gst_ref[...] = _dot_nt(wt_ref[...], u) + bst_ref[...]


def _norm_in(x2, g, w_in_t, b_small_t, bm=512):
    m = x2.shape[0]
    assert FOX_HEADS == SUBLANES and 2 * MLSTM_HEADS == SUBLANES
    return pl.pallas_call(
        _norm_in_kernel,
        grid=(m // bm,),
        in_specs=[
            pl.BlockSpec((bm, D_MODEL), lambda i: (i, 0)),
            pl.BlockSpec((1, D_MODEL), lambda i: (0, 0)),
            pl.BlockSpec((SUBLANES, D_MODEL), lambda i: (OFF_FF // SUBLANES, 0)),
            pl.BlockSpec((SUBLANES, D_MODEL), lambda i: (OFF_LI // SUBLANES, 0)),
            pl.BlockSpec((N_SMALL, 1), lambda i: (0, 0)),
        ],
        out_specs=[
            pl.BlockSpec((bm, D_MODEL), lambda i: (i, 0)),
            pl.BlockSpec((N_SMALL, bm), lambda i: (0, i)),
        ],
        out_shape=[
            jax.ShapeDtypeStruct((m, D_MODEL), BF16),
            jax.ShapeDtypeStruct((N_SMALL, m), F32),
        ],
        scratch_shapes=[pltpu.VMEM((N_SMALL, D_MODEL), BF16)],
        compiler_params=_params("arbitrary"),
        name="norm_in",
    )(x2, g, w_in_t, w_in_t, b_small_t)


CAST_ROWS = 256
EPILOGUE_ROWS = 256


def _cast_weight(dst_ref, src_fn, rows):
    def chunk(c, carry):
        sl = pl.ds(pl.multiple_of(c * CAST_ROWS, CAST_ROWS), CAST_ROWS)
        dst_ref[sl, :] = src_fn(sl).astype(BF16)
        return carry

    lax.fori_loop(0, rows // CAST_ROWS, chunk, 0)


def _in_proj_kernel(u_ref, w_ref, wx_ref, b_ref, cw_ref, cb_ref, o_ref, wb_ref, xs_ref,
                    *, bn, tiles_per_seq):
    j = pl.program_id(0)
    first_lq = OFF_FF // bn
    first_lo = (OFF_LI - FOX_HEADS) // bn

    def cast_shifted(shift):
        def body():
            def chunk(c, carry):
                src = pl.ds(pl.multiple_of(c * CAST_ROWS + shift, SUBLANES), CAST_ROWS)
                dst = pl.ds(pl.multiple_of(c * CAST_ROWS, CAST_ROWS), CAST_ROWS)
                wb_ref[dst, :] = w_ref[src, :].astype(BF16)
                return carry

            lax.fori_loop(0, bn // CAST_ROWS - 1, chunk, 0)
            last = bn - CAST_ROWS
            tail = jnp.concatenate([w_ref[last + shift:bn, :], wx_ref[0:shift, :]], axis=0)
            wb_ref[last:bn, :] = tail.astype(BF16)
        return body

    @pl.when(pl.program_id(1) == 0)
    def _():
        @pl.when(j < first_lq)
        def _():
            _cast_weight(wb_ref, lambda sl: w_ref[sl, :], bn)

        pl.when(jnp.logical_and(j >= first_lq, j < first_lo))(cast_shifted(FOX_HEADS))
        pl.when(j >= first_lo)(cast_shifted(N_SMALL))

    i = pl.program_id(1)
    bm = u_ref.shape[0]
    t_lq, t_lk, t_lo, t_mq = (c // bn for c in (COL_LQ, COL_LK, COL_LO, COL_MQ))
    is_conv = jnp.logical_or(j == t_lq, j == t_lk)
    is_gate = jnp.logical_or(j == t_lo, j > t_mq)

    row_tiles = [slice(r0, r0 + EPILOGUE_ROWS) for r0 in range(0, bm, EPILOGUE_ROWS)]

    def raw(rs):
        return _dot_nt(u_ref[rs, :], wb_ref[...]) + b_ref[...]

    @pl.when(jnp.logical_not(jnp.logical_or(is_conv, is_gate)))
    def _():
        mult = jnp.where(j == 0, FOX_Q_SCALE, jnp.where(j == t_mq, MEM_HEAD_DIM ** -0.5, 1.0))
        for rs in row_tiles:
            o_ref[rs, :] = (raw(rs) * mult).astype(BF16)

    @pl.when(is_gate)
    def _():
        for rs in row_tiles:
            o_ref[rs, :] = jax.nn.sigmoid(raw(rs)).astype(BF16)

    @pl.when(is_conv)
    def _():
        cw = cw_ref[...]
        mult = jnp.where(j == t_lk, MLSTM_HEAD_DIM ** -0.5, 1.0)
        @pl.when(i % tiles_per_seq == 0)
        def _():
            xs_ref[0, 0:SUBLANES, :] = jnp.zeros((SUBLANES, bn), F32)

        for n, rs in enumerate(row_tiles):
            cur, nxt = n % 2, (n + 1) % 2
            x = raw(rs)
            xs_ref[cur, SUBLANES:SUBLANES + EPILOGUE_ROWS, :] = x
            xs_ref[nxt, 0:SUBLANES, :] = x[EPILOGUE_ROWS - SUBLANES:EPILOGUE_ROWS, :]
            y = cb_ref[...] + cw[CONV_WIDTH - 1:CONV_WIDTH, :] * x
            for t in range(1, CONV_WIDTH):
                y = y + (cw[CONV_WIDTH - 1 - t:CONV_WIDTH - t, :]
                         * xs_ref[cur, SUBLANES - t:SUBLANES - t + EPILOGUE_ROWS, :])
            o_ref[rs, :] = (y * jax.nn.sigmoid(y) * mult).astype(BF16)


def _in_proj(u, w_in_t, b_big, conv_w, conv_b, seq, bm=2048, bn=FOX_WIDTH):
    m = u.shape[0]
    assert bn == FOX_WIDTH and OFF_FF % bn == 0 and (OFF_LI - FOX_HEADS) % bn == 0
    assert seq % bm == 0 and MLSTM_WIDTH == bn and CONV_WIDTH - 1 <= SUBLANES
    extra_per_tile = bn // N_SMALL
    t_lq = COL_LQ // bn
    return pl.pallas_call(
        functools.partial(_in_proj_kernel, bn=bn, tiles_per_seq=seq // bm),
        grid=(PROJ_BIG // bn, m // bm),
        in_specs=[
            pl.BlockSpec((bm, D_MODEL), lambda j, i: (i, 0)),
            pl.BlockSpec((bn, D_MODEL), lambda j, i: (j, 0)),
            pl.BlockSpec((N_SMALL, D_MODEL), lambda j, i: ((j + 1) * extra_per_tile, 0)),
            pl.BlockSpec((1, bn), lambda j, i: (0, j)),
            pl.BlockSpec((CONV_WIDTH, bn), lambda j, i: (0, jnp.clip(j - t_lq, 0, 1))),
            pl.BlockSpec((1, bn), lambda j, i: (0, jnp.clip(j - t_lq, 0, 1))),
        ],
        out_specs=pl.BlockSpec((bm, bn), lambda j, i: (i, j)),
        out_shape=jax.ShapeDtypeStruct((m, PROJ_BIG), BF16),
        scratch_shapes=[pltpu.VMEM((bn, D_MODEL), BF16),
                        pltpu.VMEM((2, SUBLANES + EPILOGUE_ROWS, bn), F32)],
        compiler_params=_params("arbitrary", "arbitrary"),
        name="in_proj",
    )(u, w_in_t, w_in_t, b_big, conv_w, conv_b)


def _gates_kernel(gst_ref, rows_ref, cols_ref, *, seq):
    g = gst_ref[...]
    row = lax.broadcasted_iota(jnp.int32, g.shape, 0)
    lane = lax.broadcasted_iota(jnp.int32, g.shape, 1)
    in_chunk = lane % CHUNK
    is_input_gate = jnp.logical_and(row >= ROW_MLI, row < ROW_MLB)
    local = jnp.where(is_input_gate, g, _log_sigmoid(g))
    d = 1
    while d < CHUNK:
        local = local + jnp.where(in_chunk >= d, pltpu.roll(local, d, axis=1), 0.0)
        d *= 2
    carry = pltpu.roll(jnp.where(in_chunk == CHUNK - 1, local, 0.0), 1, axis=1)
    carry = jnp.where(lane == 0, 0.0, carry)
    d = 1
    while d < CHUNK:
        carry = carry + jnp.where(in_chunk >= d, pltpu.roll(carry, d, axis=1), 0.0)
        d *= 2
    while d < seq:
        carry = carry + jnp.concatenate([jnp.zeros((N_SMALL, d), F32), carry[:, :seq - d]], axis=1)
        d *= 2
    out = jnp.where(row < ROW_MLI, local + carry, jnp.where(is_input_gate, g, local))
    rows_ref[0] = out
    padded = jnp.concatenate([out, jnp.zeros((V7X_LANES - N_SMALL, seq), F32)], axis=0)
    cols_ref[0] = padded.T


def _gates(gst, batch, seq):
    return pl.pallas_call(
        functools.partial(_gates_kernel, seq=seq),
        grid=(batch,),
        in_specs=[pl.BlockSpec((N_SMALL, seq), lambda b: (0, b))],
        out_specs=[
            pl.BlockSpec((1, N_SMALL, seq), lambda b: (b, 0, 0)),
            pl.BlockSpec((1, seq, V7X_LANES), lambda b: (b, 0, 0)),
        ],
        out_shape=[
            jax.ShapeDtypeStruct((batch, N_SMALL, seq), F32),
            jax.ShapeDtypeStruct((batch, seq, V7X_LANES), F32),
        ],
        compiler_params=_params("parallel"),
        name="gates",
    )(gst)


FOX_TQ = 1024
FOX_SUB = 128
LOG2E = 1.4426950408889634
FOX_Q_SCALE = FOX_HEAD_DIM ** -0.5 * LOG2E


def _split3(f):
    hi = f.astype(BF16).astype(F32)
    r = f - hi
    mid = r.astype(BF16).astype(F32)
    return hi, mid, r - mid


def _fox_kernel(q_ref, k_ref, v_ref, fcol_ref, o_ref, kaug_ref, vaug_ref, q2_ref, m_ref, acc_ref,
                s_ref, p_ref, alpha_ref, *, tq, sub, seq):
    h = pl.program_id(1)
    qi = pl.program_id(2)
    dh = FOX_HEAD_DIM

    def f_column(rows):
        lane = lax.broadcasted_iota(jnp.int32, (rows.shape[0], V7X_LANES), 1)
        return jnp.sum(jnp.where(lane == h, rows, 0.0), axis=-1, keepdims=True) * LOG2E

    tk = tq // 2

    @pl.when(qi == 0)
    def _():
        for c in range(seq // tk):
            sl = pl.ds(c * tk, tk)
            hi, mid, lo = _split3(f_column(fcol_ref[0, sl, :]))
            lane = lax.broadcasted_iota(jnp.int32, (tk, V7X_LANES), 1)
            aug = jnp.where(lane < 3, 1.0, jnp.where(lane == 3, -hi, jnp.where(lane == 4, -mid,
                            jnp.where(lane == 5, -lo, 0.0))))
            k2 = jnp.concatenate([k_ref[0, sl, :].astype(F32), aug], axis=1)
            kaug_ref[c] = k2.T.astype(BF16)
            vaug_ref[sl, 0:dh] = v_ref[0, sl, :]
            vaug_ref[sl, dh:2 * dh] = jnp.where(lane == 0, 1.0, 0.0).astype(BF16)

    q_start = pl.multiple_of(qi * tq, tq)
    hi, mid, lo = _split3(f_column(fcol_ref[0, pl.ds(q_start, tq), :]))
    lane = lax.broadcasted_iota(jnp.int32, (tq, V7X_LANES), 1)
    qaug = jnp.where(lane == 0, hi, jnp.where(lane == 1, mid, jnp.where(lane == 2, lo,
                     jnp.where(lane < 6, 1.0, 0.0))))
    q2_ref[:, 0:dh] = q_ref[0]
    q2_ref[:, dh:2 * dh] = qaug.astype(BF16)

    m_ref[...] = jnp.full_like(m_ref, -jnp.inf)
    acc_ref[...] = jnp.zeros_like(acc_ref)

    all_subs = [slice(r0, r0 + sub) for r0 in range(0, tq, sub)]
    low_subs = [rs for rs in all_subs if rs.start >= tk]

    def logits(t, slot, subs):
        kt = kaug_ref[t]
        for rs in subs:
            s_ref[slot, rs, :] = _dot(q2_ref[rs, :], kt)

    def softmax(slot, subs, col0=None):
        for rs in subs:
            s = s_ref[slot, rs, :]
            if col0 is not None and rs.start < col0 + tk:
                r = lax.broadcasted_iota(jnp.int32, (sub, tk), 0) + rs.start
                c = lax.broadcasted_iota(jnp.int32, (sub, tk), 1) + col0
                s = jnp.where(c <= r, s, -jnp.inf)
            m_prev = m_ref[rs, :]
            m_new = jnp.maximum(m_prev, jnp.broadcast_to(jnp.max(s, axis=-1, keepdims=True), m_prev.shape))
            alpha_ref[slot, rs, :] = jnp.exp2(m_prev - m_new)
            for c0 in range(0, tk, V7X_LANES):
                cs = slice(c0, c0 + V7X_LANES)
                p_ref[slot, rs, cs] = jnp.exp2(s[:, cs] - m_new).astype(BF16)
            m_ref[rs, :] = m_new

    def values(t, slot, subs):
        start = pl.multiple_of(t * tk, tk)
        va = vaug_ref[pl.ds(start, tk), :]
        for rs in subs:
            pv = _dot(p_ref[slot, rs, :], va)
            alpha = alpha_ref[slot, rs, :]
            for c0 in range(0, 2 * dh, V7X_LANES):
                cs = slice(c0, c0 + V7X_LANES)
                acc_ref[rs, cs] = alpha * acc_ref[rs, cs] + pv[:, cs]

    p_ref[1] = jnp.zeros((tq, tk), BF16)
    alpha_ref[1] = jnp.ones((tq, V7X_LANES), F32)
    logits(0, 0, all_subs)

    def pair(u, carry):
        t = 2 * u
        values(jnp.maximum(t - 1, 0), 1, all_subs)
        softmax(0, all_subs)
        logits(t + 1, 1, all_subs)
        values(t, 0, all_subs)
        softmax(1, all_subs)
        logits(t + 2, 0, all_subs)
        return carry

    lax.fori_loop(0, qi, pair, 0)
    t = 2 * qi
    values(jnp.maximum(t - 1, 0), 1, all_subs)
    softmax(0, all_subs, col0=0)
    logits(t + 1, 1, low_subs)
    values(t, 0, all_subs)
    softmax(1, low_subs, col0=tk)
    values(t + 1, 1, low_subs)
    o_ref[0] = (acc_ref[:, 0:dh] / acc_ref[:, dh:dh + 1]).astype(BF16)


def _fox(proj3, cols, tq=FOX_TQ, sub=FOX_SUB):
    batch, seq, _ = proj3.shape
    dh = FOX_HEAD_DIM
    tk = tq // 2
    return pl.pallas_call(
        functools.partial(_fox_kernel, tq=tq, sub=sub, seq=seq),
        grid=(batch, FOX_HEADS, seq // tq),
        in_specs=[
            pl.BlockSpec((1, tq, dh), lambda b, h, i: (b, i, h)),
            pl.BlockSpec((1, seq, dh), lambda b, h, i: (b, 0, FOX_HEADS + h)),
            pl.BlockSpec((1, seq, dh), lambda b, h, i: (b, 0, 2 * FOX_HEADS + h)),
            pl.BlockSpec((1, seq, V7X_LANES), lambda b, h, i: (b, 0, 0)),
        ],
        out_specs=pl.BlockSpec((1, tq, dh), lambda b, h, i: (b, i, h)),
        out_shape=jax.ShapeDtypeStruct((batch, seq, FOX_WIDTH), BF16),
        scratch_shapes=[pltpu.VMEM((seq // tk, 2 * dh, tk), BF16), pltpu.VMEM((seq, 2 * dh), BF16),
                        pltpu.VMEM((tq, 2 * dh), BF16), pltpu.VMEM((tq, V7X_LANES), F32),
                        pltpu.VMEM((tq, 2 * dh), F32), pltpu.VMEM((2, tq, tk), F32),
                        pltpu.VMEM((2, tq, tk), BF16), pltpu.VMEM((2, tq, V7X_LANES), F32)],
        compiler_params=_params("parallel", "parallel", "arbitrary"),
        name="fox_attn",
    )(proj3, proj3, proj3, cols)


def _mlstm_kernel(q_ref, k_ref, v_ref, o_ref, gn_ref, rows_ref, cols_ref, w0_ref, w1_ref, w2_ref,
                  w3_ref, y_ref, w0b_ref, w1b_ref, w2b_ref, w3b_ref, c_ref, n_ref, m_ref, *, batch):
    j = pl.program_id(0)
    L = CHUNK
    dh = MLSTM_HEAD_DIM

    @pl.when(j == 0)
    def _():
        c_ref[...] = jnp.zeros_like(c_ref)
        n_ref[...] = jnp.zeros_like(n_ref)
        m_ref[...] = jnp.zeros_like(m_ref)

    for src, dst in ((w0_ref, w0b_ref), (w1_ref, w1b_ref), (w2_ref, w2b_ref), (w3_ref, w3b_ref)):
        dst[...] = src[...].astype(BF16)

    lanes = V7X_LANES
    assert L == lanes and dh % lanes == 0
    halves = [slice(c0, c0 + lanes) for c0 in range(0, dh, lanes)]
    r = lax.broadcasted_iota(jnp.int32, (L, L), 0)
    c = lax.broadcasted_iota(jnp.int32, (L, L), 1)
    causal = c <= r
    chains = [(b, h) for b in range(batch) for h in range(MLSTM_HEADS)]

    def rep(x):
        return jnp.broadcast_to(x, (L, lanes))

    sel_r = lax.broadcasted_iota(jnp.int32, (3 * lanes, 2 * lanes), 0) % lanes
    sel_c = lax.broadcasted_iota(jnp.int32, (3 * lanes, 2 * lanes), 1)
    pieces = []
    for b in range(batch):
        hi, mid, lo = _split3(cols_ref[b])
        pieces.append(jnp.concatenate([hi, mid, lo], axis=1).astype(BF16))

    st1 = []
    for b, h in chains:
        hs = slice(h * dh, (h + 1) * dh)
        st = b * MLSTM_HEADS + h
        sel = sel_r == jnp.where(sel_c < lanes, ROW_MLB + h, ROW_MLI + h)
        bi = _dot(pieces[b], jnp.where(sel, 1.0, 0.0).astype(BF16))
        bcol, icol = bi[:, 0:lanes], bi[:, lanes:2 * lanes]
        qb = q_ref[b, :, hs]
        kb = k_ref[b, :, hs]
        brow = rows_ref[b, ROW_MLB + h:ROW_MLB + h + 1, :]
        irow = rows_ref[b, ROW_MLI + h:ROW_MLI + h + 1, :]
        m_prev = m_ref[st]
        c_prev = c_ref[st]
        qk_raw = _dot_nt(qb, kb)
        qc = _dot(qb, c_prev.astype(BF16))

        dlog = jnp.where(causal, bcol - brow + irow, -jnp.inf)
        inter = bcol + m_prev
        m_t = jnp.maximum(inter, rep(jnp.max(dlog, axis=-1, keepdims=True)))
        w = jnp.exp(dlog - m_t)
        a = jnp.exp(inter - m_t)
        b_end = brow[:, L - 1:L]
        m_new = jnp.maximum(b_end + m_prev, jnp.max(b_end - brow + irow, axis=-1, keepdims=True))
        decay = jnp.exp(b_end + m_prev - m_new)
        ws = jnp.exp(b_end - bcol + icol - m_new)
        qn = rep(jnp.sum(qb.astype(F32) * n_ref[st], axis=-1, keepdims=True))
        st1.append((hs, st, kb, c_prev, qk_raw, qc, m_t, w, a, m_new, decay, ws, qn))

    st2 = []
    for (b, h), (hs, st, kb, c_prev, qk_raw, qc, m_t, w, a, m_new, decay, ws, qn) in zip(chains, st1):
        v = v_ref[b, :, hs]
        qk = qk_raw * w
        k = kb.astype(F32)
        kw = jnp.concatenate([k[:, cs] * ws for cs in halves], axis=1)
        pv = _dot(qk.astype(BF16), v)
        upd = _dot_tn(kw.astype(BF16), v)
        st2.append((qk, kw, pv, upd))

    for (b, h), s1, (qk, kw, pv, upd) in zip(chains, st1, st2):
        hs, st, kb, c_prev, qk_raw, qc, m_t, w, a, m_new, decay, ws, qn = s1
        den = a * qn + rep(jnp.sum(qk, axis=-1, keepdims=True))
        inv = 1.0 / jnp.maximum(jnp.abs(den), jnp.exp(-m_t))
        h_parts = [(a * qc[:, cs] + pv[:, cs]) * inv for cs in halves]
        sq = h_parts[0] * h_parts[0]
        for hp in h_parts[1:]:
            sq = sq + hp * hp
        rms = lax.rsqrt(rep(jnp.sum(sq, axis=-1, keepdims=True)) * (1.0 / dh) + EPS)
        for cs, hp in zip(halves, h_parts):
            oc = slice(hs.start + cs.start, hs.start + cs.stop)
            y = hp * rms * gn_ref[:, oc] * o_ref[b, :, oc].astype(F32)
            y_ref[b, :, oc] = y.astype(BF16)
        c_ref[st] = decay * c_prev + upd
        n_ref[st] = decay * n_ref[st] + jnp.sum(kw, axis=0, keepdims=True)
        m_ref[st] = m_new


def _mlstm(proj3, gn, rows, cols, weights):
    batch, seq, _ = proj3.shape
    dh = MLSTM_HEAD_DIM
    width = MLSTM_WIDTH
    nc = seq // CHUNK
    chains = batch * MLSTM_HEADS

    def col(base):
        return lambda j: (0, j, base // width)

    shares = [w.shape[0] // nc for w in weights]
    assert all(s * nc == w.shape[0] and s % 16 == 0 for s, w in zip(shares, weights))
    w_specs = [pl.BlockSpec((s, w.shape[1]), lambda j: (j, 0)) for s, w in zip(shares, weights)]

    return pl.pallas_call(
        functools.partial(_mlstm_kernel, batch=batch),
        grid=(nc,),
        in_specs=[
            pl.BlockSpec((batch, CHUNK, width), col(COL_LQ)),
            pl.BlockSpec((batch, CHUNK, width), col(COL_LK)),
            pl.BlockSpec((batch, CHUNK, width), col(COL_LV)),
            pl.BlockSpec((batch, CHUNK, width), col(COL_LO)),
            pl.BlockSpec((1, width), lambda j: (0, 0)),
            pl.BlockSpec((batch, N_SMALL, CHUNK), lambda j: (0, 0, j)),
            pl.BlockSpec((batch, CHUNK, V7X_LANES), lambda j: (0, j, 0)),
        ] + w_specs,
        out_specs=[pl.BlockSpec((batch, CHUNK, width), lambda j: (0, j, 0))] + w_specs,
        out_shape=[jax.ShapeDtypeStruct((batch, seq, width), BF16)]
        + [jax.ShapeDtypeStruct(w.shape, BF16) for w in weights],
        scratch_shapes=[pltpu.VMEM((chains, dh, dh), F32), pltpu.VMEM((chains, 1, dh), F32),
                        pltpu.VMEM((chains, 1, 1), F32)],
        compiler_params=_params("arbitrary"),
        name="mlstm",
    )(proj3, proj3, proj3, proj3, gn, rows, cols, *weights)


def _mem_kv_kernel(mem_ref, g_ref, w_ref, o_ref):
    u = _rms(mem_ref[...], g_ref[...]).astype(BF16)
    o_ref[...] = _dot(u, w_ref[...].astype(BF16)).astype(BF16)


def _mem_kv(mem2, g, w, bn=512):
    m = mem2.shape[0]
    n = w.shape[1]
    return pl.pallas_call(
        _mem_kv_kernel,
        grid=(n // bn,),
        in_specs=[
            pl.BlockSpec((m, D_MODEL), lambda j: (0, 0)),
            pl.BlockSpec((1, D_MODEL), lambda j: (0, 0)),
            pl.BlockSpec((D_MODEL, bn), lambda j: (0, j)),
        ],
        out_specs=pl.BlockSpec((m, bn), lambda j: (0, j)),
        out_shape=jax.ShapeDtypeStruct((m, n), BF16),
        compiler_params=_params("parallel"),
        name="mem_kv",
    )(mem2, g, w)


def _mem_attn_kernel(q_ref, k_ref, v_ref, o_ref):
    dh = MEM_HEAD_DIM
    for hh in range(MEM_HEADS):
        sl = slice(hh * dh, (hh + 1) * dh)
        s = _dot_nt(q_ref[0, :, sl], k_ref[0, :, sl])
        p = jnp.exp(s - jnp.max(s, axis=-1, keepdims=True))
        l = jnp.sum(p, axis=-1, keepdims=True)
        o_ref[0, :, sl] = (_dot(p.astype(BF16), v_ref[0, :, sl]) / l).astype(BF16)


def _mem_attn(proj3, mkv3, tq=512):
    batch, seq, _ = proj3.shape
    mem_len = mkv3.shape[1]
    return pl.pallas_call(
        _mem_attn_kernel,
        grid=(batch, seq // tq),
        in_specs=[
            pl.BlockSpec((1, tq, MEM_WIDTH), lambda b, i: (b, i, COL_MQ // MEM_WIDTH)),
            pl.BlockSpec((1, mem_len, MEM_WIDTH), lambda b, i: (b, 0, 0)),
            pl.BlockSpec((1, mem_len, MEM_WIDTH), lambda b, i: (b, 0, 1)),
        ],
        out_specs=pl.BlockSpec((1, tq, MEM_WIDTH), lambda b, i: (b, i, 0)),
        out_shape=jax.ShapeDtypeStruct((batch, seq, MEM_WIDTH), BF16),
        compiler_params=_params("parallel", "parallel"),
        name="mem_attn",
    )(proj3, mkv3, mkv3)


MERGE_ROWS = 256


def _merge_out_kernel(yf_ref, ym_ref, yc_ref, g0_ref, g1_ref, g2_ref, wf_ref, wm_ref, wc_ref, wo_ref,
                      x_ref, gn_ref, h_ref, u_ref):
    for r0 in range(0, x_ref.shape[0], MERGE_ROWS):
        rs = slice(r0, r0 + MERGE_ROWS)
        merged = g0_ref[rs, :].astype(F32) * _dot(yf_ref[rs, :], wf_ref[...])
        merged = merged + g1_ref[rs, :].astype(F32) * _dot(ym_ref[rs, :], wm_ref[...])
        merged = merged + g2_ref[rs, :].astype(F32) * _dot(yc_ref[rs, :], wc_ref[...])
        h = x_ref[rs, :] + _dot(merged.astype(BF16), wo_ref[...])
        h_ref[rs, :] = h
        u_ref[rs, :] = _rms(h, gn_ref[...]).astype(BF16)


def _merge_out(y_fox, y_ml, y_mem, proj, w_f, w_m, w_c, w_o, x2, g_ffn, bm=256):
    m = x2.shape[0]
    kdim = y_fox.shape[1]
    y_spec = pl.BlockSpec((bm, kdim), lambda i: (i, 0))
    row_spec = pl.BlockSpec((bm, D_MODEL), lambda i: (i, 0))

    def gate_spec(branch):
        return pl.BlockSpec((bm, D_MODEL), lambda i: (i, COL_GATES // D_MODEL + branch))

    def resident(rows):
        return pl.BlockSpec((rows, D_MODEL), lambda i: (0, 0), pipeline_mode=pl.Buffered(1))

    return pl.pallas_call(
        _merge_out_kernel,
        grid=(m // bm,),
        in_specs=[y_spec, y_spec, y_spec, gate_spec(0), gate_spec(1), gate_spec(2),
                  resident(kdim), resident(kdim), resident(kdim), resident(D_MODEL),
                  row_spec, pl.BlockSpec((1, D_MODEL), lambda i: (0, 0))],
        out_specs=[row_spec, row_spec],
        out_shape=[
            jax.ShapeDtypeStruct((m, D_MODEL), F32),
            jax.ShapeDtypeStruct((m, D_MODEL), BF16),
        ],
        compiler_params=_params("parallel"),
        name="merge_out",
    )(y_fox, y_ml, y_mem, proj, proj, proj, w_f, w_m, w_c, w_o, x2, g_ffn)


def _ffn_in_kernel(u_ref, wg_ref, wu_ref, wo_ref, o_ref, wob_ref, wgb_ref, wub_ref):
    @pl.when(pl.program_id(1) == 0)
    def _():
        _cast_weight(wgb_ref, lambda sl: wg_ref[sl, :], D_MODEL)
        _cast_weight(wub_ref, lambda sl: wu_ref[sl, :], D_MODEL)

    wob_ref[...] = wo_ref[...].astype(BF16)
    u = u_ref[...]
    gate = _dot(u, wgb_ref[...])
    up = _dot(u, wub_ref[...])
    o_ref[...] = (gate * jax.nn.sigmoid(gate) * up).astype(BF16)


def _ffn_in(u, w, w_out_f32, bm=1024, bn=512):
    m = u.shape[0]
    nb = D_FF // bn
    mt = m // bm
    share = D_FF // (nb * mt)
    assert share * nb * mt == D_FF and share % 16 == 0
    return pl.pallas_call(
        _ffn_in_kernel,
        grid=(nb, mt),
        in_specs=[
            pl.BlockSpec((bm, D_MODEL), lambda j, i: (i, 0)),
            pl.BlockSpec((D_MODEL, bn), lambda j, i: (0, j)),
            pl.BlockSpec((D_MODEL, bn), lambda j, i: (0, nb + j)),
            pl.BlockSpec((share, D_MODEL), lambda j, i: (j * mt + i, 0)),
        ],
        out_specs=[
            pl.BlockSpec((bm, bn), lambda j, i: (i, j)),
            pl.BlockSpec((share, D_MODEL), lambda j, i: (j * mt + i, 0)),
        ],
        out_shape=[
            jax.ShapeDtypeStruct((m, D_FF), BF16),
            jax.ShapeDtypeStruct((D_FF, D_MODEL), BF16),
        ],
        scratch_shapes=[pltpu.VMEM((D_MODEL, bn), BF16)] * 2,
        compiler_params=_params("arbitrary", "arbitrary"),
        name="ffn_in",
    )(u, w, w, w_out_f32)


def _ffn_out_kernel(a_ref, w_ref, r_ref, g_ref, o_ref, *, final_norm):
    h = r_ref[...] + _dot(a_ref[...], w_ref[...])
    o_ref[...] = _rms(h, g_ref[...]) if final_norm else h


def _ffn_out(act, w_bf16, resid, final_gain, bm=256):
    m = act.shape[0]
    final_norm = final_gain is not None
    gain = final_gain if final_norm else jnp.ones((1, D_MODEL), F32)
    return pl.pallas_call(
        functools.partial(_ffn_out_kernel, final_norm=final_norm),
        grid=(m // bm,),
        in_specs=[
            pl.BlockSpec((bm, D_FF), lambda i: (i, 0)),
            pl.BlockSpec((D_FF, D_MODEL), lambda i: (0, 0), pipeline_mode=pl.Buffered(1)),
            pl.BlockSpec((bm, D_MODEL), lambda i: (i, 0)),
            pl.BlockSpec((1, D_MODEL), lambda i: (0, 0)),
        ],
        out_specs=pl.BlockSpec((bm, D_MODEL), lambda i: (i, 0)),
        out_shape=jax.ShapeDtypeStruct((m, D_MODEL), F32),
        compiler_params=_params("parallel"),
        name="ffn_out",
    )(act, w_bf16, resid, gain)


def _layer(h2, mem2, batch, seq, norm_mix, w_in, b_in, conv_w, conv_b, mlstm_norm, norm_mem,
           w_mem_kv, w_br_fox, w_br_mlstm, w_br_mem, w_out, norm_ffn, w_ffn_in, w_ffn_out,
           final_gain):
    b_big = jnp.concatenate([b_in[:OFF_FF], b_in[OFF_LQ:OFF_LI], b_in[OFF_LO:]])[None, :]
    b_small_t = jnp.concatenate([b_in[OFF_FF:OFF_LQ], b_in[OFF_LI:OFF_LO]])[:, None]

    w_in_t = w_in.T
    u, gst = _norm_in(h2, norm_mix[None, :], w_in_t, b_small_t)
    proj = _in_proj(u, w_in_t, b_big, conv_w, conv_b[None, :], seq)
    rows, cols = _gates(gst, batch, seq)
    proj3 = proj.reshape(batch, seq, PROJ_BIG)

    y_fox = _fox(proj3, cols)
    y_ml, wf_b, wm_b, wc_b, wo_b = _mlstm(proj3, mlstm_norm[None, :], rows, cols,
                                          (w_br_fox, w_br_mlstm, w_br_mem, w_out))
    mkv = _mem_kv(mem2, norm_mem[None, :], w_mem_kv)
    y_mem = _mem_attn(proj3, mkv.reshape(batch, -1, 2 * MEM_WIDTH))

    tokens = batch * seq
    h2, u_ffn = _merge_out(y_fox.reshape(tokens, FOX_WIDTH), y_ml.reshape(tokens, MLSTM_WIDTH),
                           y_mem.reshape(tokens, MEM_WIDTH), proj, wf_b, wm_b, wc_b, wo_b,
                           h2, norm_ffn[None, :])
    act, w_ffn_out_bf16 = _ffn_in(u_ffn, w_ffn_in, w_ffn_out)
    return _ffn_out(act, w_ffn_out_bf16, h2, final_gain)


def kernel(x, mem, norm_mix, w_in, b_in, conv_w, conv_b, mlstm_norm, norm_mem, w_mem_kv, w_br_fox,
           w_br_mlstm, w_br_mem, w_out, norm_ffn, w_ffn_in, w_ffn_out, norm_final):
    batch, seq, d = x.shape
    assert d == D_MODEL and seq % CHUNK == 0
    h2 = x.reshape(batch * seq, d)
    mem2 = mem.reshape(batch * mem.shape[1], d)
    depth = norm_mix.shape[0]
    for l in range(depth):
        final_gain = norm_final[None, :] if l == depth - 1 else None
        h2 = _layer(h2, mem2, batch, seq, norm_mix[l], w_in[l], b_in[l], conv_w[l], conv_b[l],
                    mlstm_norm[l], norm_mem[l], w_mem_kv[l], w_br_fox[l], w_br_mlstm[l],
                    w_br_mem[l], w_out[l], norm_ffn[l], w_ffn_in[l], w_ffn_out[l], final_gain)
    return h2.reshape(batch, seq, d)
```

```python
import functools

import jax
import jax.numpy as jnp
from jax import lax
from jax.experimental import pallas as pl
from jax.experimental.pallas import tpu as pltpu

D_MODEL = 2048
FOX_HEADS = 8
FOX_HEAD_DIM = 128
FOX_WIDTH = FOX_HEADS * FOX_HEAD_DIM
MLSTM_HEADS = 4
MLSTM_HEAD_DIM = 256
MLSTM_WIDTH = MLSTM_HEADS * MLSTM_HEAD_DIM
MEM_HEADS = 4
MEM_HEAD_DIM = 256
MEM_WIDTH = MEM_HEADS * MEM_HEAD_DIM
N_BRANCH = 3
CONV_WIDTH = 4
CHUNK = 128
D_FF = 5632
EPS = 1e-6

OFF_FF = 3 * FOX_WIDTH
OFF_LQ = OFF_FF + FOX_HEADS
OFF_LI = OFF_LQ + 3 * MLSTM_WIDTH
OFF_LO = OFF_LI + 2 * MLSTM_HEADS
N_SMALL = FOX_HEADS + 2 * MLSTM_HEADS
PROJ_BIG = 3 * FOX_WIDTH + 3 * MLSTM_WIDTH + MLSTM_WIDTH + MEM_WIDTH + N_BRANCH * D_MODEL

COL_LQ = 3 * FOX_WIDTH
COL_LK = COL_LQ + MLSTM_WIDTH
COL_LV = COL_LK + MLSTM_WIDTH
COL_LO = COL_LV + MLSTM_WIDTH
COL_MQ = COL_LO + MLSTM_WIDTH
COL_GATES = COL_MQ + MEM_WIDTH

ROW_FOX = 0
ROW_MLI = FOX_HEADS
ROW_MLB = FOX_HEADS + MLSTM_HEADS

V7X_LANES = 128
V7X_VMEM_LIMIT = 56 * 1024 * 1024

F32 = jnp.float32
BF16 = jnp.bfloat16


def _params(*sem):
    return pltpu.CompilerParams(dimension_semantics=sem, vmem_limit_bytes=V7X_VMEM_LIMIT)


def _rms(x, g):
    return x * lax.rsqrt(jnp.mean(x * x, axis=-1, keepdims=True) + EPS) * g


def _dot(a, b):
    return jnp.dot(a, b, preferred_element_type=F32)


def _dot_nt(a, b):
    return lax.dot_general(a, b, (((1,), (1,)), ((), ())), preferred_element_type=F32)


def _dot_tn(a, b):
    return lax.dot_general(a, b, (((0,), (0,)), ((), ())), preferred_element_type=F32)


def _log_sigmoid(x):
    return jnp.minimum(x, 0.0) - jnp.log1p(jnp.exp(-jnp.abs(x)))


SUBLANES = 8


def _norm_in_kernel(x_ref, g_ref, wf_ref, wl_ref, bst_ref, u_ref, gst_ref, wt_ref):
    @pl.when(pl.program_id(0) == 0)
    def _():
        wt_ref[...] = jnp.concatenate([wf_ref[...], wl_ref[...]], axis=0).astype(BF16)

    u = _rms(x_ref[...], g_ref[...]).astype(BF16)
    u_ref[...] = u
    gst_ref[...] = _dot_nt(wt_ref[...], u) + bst_ref[...]


def _norm_in(x2, g, w_in_t, b_small_t, bm=512):
    m = x2.shape[0]
    assert FOX_HEADS == SUBLANES and 2 * MLSTM_HEADS == SUBLANES
    return pl.pallas_call(
        _norm_in_kernel,
        grid=(m // bm,),
        in_specs=[
            pl.BlockSpec((bm, D_MODEL), lambda i: (i, 0)),
            pl.BlockSpec((1, D_MODEL), lambda i: (0, 0)),
            pl.BlockSpec((SUBLANES, D_MODEL), lambda i: (OFF_FF // SUBLANES, 0)),
            pl.BlockSpec((SUBLANES, D_MODEL), lambda i: (OFF_LI // SUBLANES, 0)),
            pl.BlockSpec((N_SMALL, 1), lambda i: (0, 0)),
        ],
        out_specs=[
            pl.BlockSpec((bm, D_MODEL), lambda i: (i, 0)),
            pl.BlockSpec((N_SMALL, bm), lambda i: (0, i)),
        ],
        out_shape=[
            jax.ShapeDtypeStruct((m, D_MODEL), BF16),
            jax.ShapeDtypeStruct((N_SMALL, m), F32),
        ],
        scratch_shapes=[pltpu.VMEM((N_SMALL, D_MODEL), BF16)],
        compiler_params=_params("arbitrary"),
        name="norm_in",
    )(x2, g, w_in_t, w_in_t, b_small_t)


CAST_ROWS = 256
EPILOGUE_ROWS = 256


def _cast_weight(dst_ref, src_fn, rows):
    def chunk(c, carry):
        sl = pl.ds(pl.multiple_of(c * CAST_ROWS, CAST_ROWS), CAST_ROWS)
        dst_ref[sl, :] = src_fn(sl).astype(BF16)
        return carry

    lax.fori_loop(0, rows // CAST_ROWS, chunk, 0)


def _in_proj_kernel(u_ref, w_ref, wx_ref, b_ref, cw_ref, cb_ref, o_ref, wb_ref, xs_ref,
                    *, bn, tiles_per_seq):
    j = pl.program_id(0)
    first_lq = OFF_FF // bn
    first_lo = (OFF_LI - FOX_HEADS) // bn

    def cast_shifted(shift):
        def body():
            def chunk(c, carry):
                src = pl.ds(pl.multiple_of(c * CAST_ROWS + shift, SUBLANES), CAST_ROWS)
                dst = pl.ds(pl.multiple_of(c * CAST_ROWS, CAST_ROWS), CAST_ROWS)
                wb_ref[dst, :] = w_ref[src, :].astype(BF16)
                return carry

            lax.fori_loop(0, bn // CAST_ROWS - 1, chunk, 0)
            last = bn - CAST_ROWS
            tail = jnp.concatenate([w_ref[last + shift:bn, :], wx_ref[0:shift, :]], axis=0)
            wb_ref[last:bn, :] = tail.astype(BF16)
        return body

    @pl.when(pl.program_id(1) == 0)
    def _():
        @pl.when(j < first_lq)
        def _():
            _cast_weight(wb_ref, lambda sl: w_ref[sl, :], bn)

        pl.when(jnp.logical_and(j >= first_lq, j < first_lo))(cast_shifted(FOX_HEADS))
        pl.when(j >= first_lo)(cast_shifted(N_SMALL))

    i = pl.program_id(1)
    bm = u_ref.shape[0]
    t_lq, t_lk, t_lo, t_mq = (c // bn for c in (COL_LQ, COL_LK, COL_LO, COL_MQ))
    is_conv = jnp.logical_or(j == t_lq, j == t_lk)
    is_gate = jnp.logical_or(j == t_lo, j > t_mq)

    row_tiles = [slice(r0, r0 + EPILOGUE_ROWS) for r0 in range(0, bm, EPILOGUE_ROWS)]

    def raw(rs):
        return _dot_nt(u_ref[rs, :], wb_ref[...]) + b_ref[...]

    @pl.when(jnp.logical_not(jnp.logical_or(is_conv, is_gate)))
    def _():
        mult = jnp.where(j == 0, FOX_Q_SCALE, jnp.where(j == t_mq, MEM_HEAD_DIM ** -0.5, 1.0))
        for rs in row_tiles:
            o_ref[rs, :] = (raw(rs) * mult).astype(BF16)

    @pl.when(is_gate)
    def _():
        for rs in row_tiles:
            o_ref[rs, :] = jax.nn.sigmoid(raw(rs)).astype(BF16)

    @pl.when(is_conv)
    def _():
        cw = cw_ref[...]
        mult = jnp.where(j == t_lk, MLSTM_HEAD_DIM ** -0.5, 1.0)
        @pl.when(i % tiles_per_seq == 0)
        def _():
            xs_ref[0, 0:SUBLANES, :] = jnp.zeros((SUBLANES, bn), F32)

        for n, rs in enumerate(row_tiles):
            cur, nxt = n % 2, (n + 1) % 2
            x = raw(rs)
            xs_ref[cur, SUBLANES:SUBLANES + EPILOGUE_ROWS, :] = x
            xs_ref[nxt, 0:SUBLANES, :] = x[EPILOGUE_ROWS - SUBLANES:EPILOGUE_ROWS, :]
            y = cb_ref[...] + cw[CONV_WIDTH - 1:CONV_WIDTH, :] * x
            for t in range(1, CONV_WIDTH):
                y = y + (cw[CONV_WIDTH - 1 - t:CONV_WIDTH - t, :]
                         * xs_ref[cur, SUBLANES - t:SUBLANES - t + EPILOGUE_ROWS, :])
            o_ref[rs, :] = (y * jax.nn.sigmoid(y) * mult).astype(BF16)


def _in_proj(u, w_in_t, b_big, conv_w, conv_b, seq, bm=2048, bn=FOX_WIDTH):
    m = u.shape[0]
    assert bn == FOX_WIDTH and OFF_FF % bn == 0 and (OFF_LI - FOX_HEADS) % bn == 0
    assert seq % bm == 0 and MLSTM_WIDTH == bn and CONV_WIDTH - 1 <= SUBLANES
    extra_per_tile = bn // N_SMALL
    t_lq = COL_LQ // bn
    return pl.pallas_call(
        functools.partial(_in_proj_kernel, bn=bn, tiles_per_seq=seq // bm),
        grid=(PROJ_BIG // bn, m // bm),
        in_specs=[
            pl.BlockSpec((bm, D_MODEL), lambda j, i: (i, 0)),
            pl.BlockSpec((bn, D_MODEL), lambda j, i: (j, 0)),
            pl.BlockSpec((N_SMALL, D_MODEL), lambda j, i: ((j + 1) * extra_per_tile, 0)),
            pl.BlockSpec((1, bn), lambda j, i: (0, j)),
            pl.BlockSpec((CONV_WIDTH, bn), lambda j, i: (0, jnp.clip(j - t_lq, 0, 1))),
            pl.BlockSpec((1, bn), lambda j, i: (0, jnp.clip(j - t_lq, 0, 1))),
        ],
        out_specs=pl.BlockSpec((bm, bn), lambda j, i: (i, j)),
        out_shape=jax.ShapeDtypeStruct((m, PROJ_BIG), BF16),
        scratch_shapes=[pltpu.VMEM((bn, D_MODEL), BF16),
                        pltpu.VMEM((2, SUBLANES + EPILOGUE_ROWS, bn), F32)],
        compiler_params=_params("arbitrary", "arbitrary"),
        name="in_proj",
    )(u, w_in_t, w_in_t, b_big, conv_w, conv_b)


def _gates_kernel(gst_ref, rows_ref, cols_ref, *, seq):
    g = gst_ref[...]
    row = lax.broadcasted_iota(jnp.int32, g.shape, 0)
    lane = lax.broadcasted_iota(jnp.int32, g.shape, 1)
    in_chunk = lane % CHUNK
    is_input_gate = jnp.logical_and(row >= ROW_MLI, row < ROW_MLB)
    local = jnp.where(is_input_gate, g, _log_sigmoid(g))
    d = 1
    while d < CHUNK:
        local = local + jnp.where(in_chunk >= d, pltpu.roll(local, d, axis=1), 0.0)
        d *= 2
    carry = pltpu.roll(jnp.where(in_chunk == CHUNK - 1, local, 0.0), 1, axis=1)
    carry = jnp.where(lane == 0, 0.0, carry)
    d = 1
    while d < CHUNK:
        carry = carry + jnp.where(in_chunk >= d, pltpu.roll(carry, d, axis=1), 0.0)
        d *= 2
    while d < seq:
        carry = carry + jnp.concatenate([jnp.zeros((N_SMALL, d), F32), carry[:, :seq - d]], axis=1)
        d *= 2
    out = jnp.where(row < ROW_MLI, local + carry, jnp.where(is_input_gate, g, local))
    rows_ref[0] = out
    padded = jnp.concatenate([out, jnp.zeros((V7X_LANES - N_SMALL, seq), F32)], axis=0)
    cols_ref[0] = padded.T


def _gates(gst, batch, seq):
    return pl.pallas_call(
        functools.partial(_gates_kernel, seq=seq),
        grid=(batch,),
        in_specs=[pl.BlockSpec((N_SMALL, seq), lambda b: (0, b))],
        out_specs=[
            pl.BlockSpec((1, N_SMALL, seq), lambda b: (b, 0, 0)),
            pl.BlockSpec((1, seq, V7X_LANES), lambda b: (b, 0, 0)),
        ],
        out_shape=[
            jax.ShapeDtypeStruct((batch, N_SMALL, seq), F32),
            jax.ShapeDtypeStruct((batch, seq, V7X_LANES), F32),
        ],
        compiler_params=_params("parallel"),
        name="gates",
    )(gst)


FOX_TQ = 1024
FOX_SUB = 128
LOG2E = 1.4426950408889634
FOX_Q_SCALE = FOX_HEAD_DIM ** -0.5 * LOG2E


def _split3(f):
    hi = f.astype(BF16).astype(F32)
    r = f - hi
    mid = r.astype(BF16).astype(F32)
    return hi, mid, r - mid


def _fox_kernel(q_ref, k_ref, v_ref, fcol_ref, o_ref, kaug_ref, vaug_ref, q2_ref, m_ref, acc_ref,
                s_ref, p_ref, alpha_ref, *, tq, sub, seq):
    h = pl.program_id(1)
    dh = FOX_HEAD_DIM
    tk = tq // 2
    nq = seq // tq

    def f_column(rows):
        lane = lax.broadcasted_iota(jnp.int32, (rows.shape[0], V7X_LANES), 1)
        return jnp.sum(jnp.where(lane == h, rows, 0.0), axis=-1, keepdims=True) * LOG2E

    for c in range(seq // tk):
        sl = slice(c * tk, (c + 1) * tk)
        hi, mid, lo = _split3(f_column(fcol_ref[0, sl, :]))
        lane = lax.broadcasted_iota(jnp.int32, (tk, V7X_LANES), 1)
        kaug = jnp.where(lane < 3, 1.0, jnp.where(lane == 3, -hi, jnp.where(lane == 4, -mid,
                         jnp.where(lane == 5, -lo, 0.0))))
        qaug = jnp.where(lane == 0, hi, jnp.where(lane == 1, mid, jnp.where(lane == 2, lo,
                         jnp.where(lane < 6, 1.0, 0.0))))
        k2 = jnp.concatenate([k_ref[0, sl, :].astype(F32), kaug], axis=1)
        kaug_ref[c] = k2.T.astype(BF16)
        vaug_ref[sl, 0:dh] = v_ref[0, sl, :]
        vaug_ref[sl, dh:2 * dh] = jnp.where(lane == 0, 1.0, 0.0).astype(BF16)
        q2_ref[sl, 0:dh] = q_ref[0, sl, :]
        q2_ref[sl, dh:2 * dh] = qaug.astype(BF16)

    m_ref[...] = jnp.full_like(m_ref, -jnp.inf)
    acc_ref[...] = jnp.zeros_like(acc_ref)

    all_subs = [slice(r0, r0 + sub) for r0 in range(0, tq, sub)]
    low_subs = [rs for rs in all_subs if rs.start >= tk]
    steps = []
    for qi in range(nq):
        steps += [(qi, t, None, all_subs) for t in range(2 * qi)]
        steps += [(qi, 2 * qi, 0, all_subs), (qi, 2 * qi + 1, tk, low_subs)]

    def logits(n):
        qi, t, _, subs = steps[n]
        kt = kaug_ref[t]
        for rs in subs:
            s_ref[n % 2, rs, :] = _dot(q2_ref[qi * tq + rs.start:qi * tq + rs.stop, :], kt)

    def softmax(n):
        qi, _, col0, subs = steps[n]
        slot = n % 2
        for rs in subs:
            s = s_ref[slot, rs, :]
            if col0 is not None and rs.start < col0 + tk:
                r = lax.broadcasted_iota(jnp.int32, (sub, tk), 0) + rs.start
                c = lax.broadcasted_iota(jnp.int32, (sub, tk), 1) + col0
                s = jnp.where(c <= r, s, -jnp.inf)
            m_prev = m_ref[qi, rs, :]
            m_new = jnp.maximum(m_prev, jnp.broadcast_to(jnp.max(s, axis=-1, keepdims=True), m_prev.shape))
            alpha_ref[slot, rs, :] = jnp.exp2(m_prev - m_new)
            for c0 in range(0, tk, V7X_LANES):
                cs = slice(c0, c0 + V7X_LANES)
                p_ref[slot, rs, cs] = jnp.exp2(s[:, cs] - m_new).astype(BF16)
            m_ref[qi, rs, :] = m_new

    def values(n):
        qi, t, _, subs = steps[n]
        slot = n % 2
        va = vaug_ref[t * tk:(t + 1) * tk, :]
        for rs in subs:
            pv = _dot(p_ref[slot, rs, :], va)
            alpha = alpha_ref[slot, rs, :]
            for c0 in range(0, 2 * dh, V7X_LANES):
                cs = slice(c0, c0 + V7X_LANES)
                acc_ref[qi, rs, cs] = alpha * acc_ref[qi, rs, cs] + pv[:, cs]
        if n + 1 == len(steps) or steps[n + 1][0] != qi:
            o_ref[0, qi * tq:(qi + 1) * tq, :] = (acc_ref[qi, :, 0:dh]
                                                  / acc_ref[qi, :, dh:dh + 1]).astype(BF16)

    logits(0)
    for n in range(len(steps)):
        if n > 0:
            values(n - 1)
        softmax(n)
        if n + 1 < len(steps):
            logits(n + 1)
    values(len(steps) - 1)


def _fox(proj3, cols, tq=FOX_TQ, sub=FOX_SUB):
    batch, seq, _ = proj3.shape
    dh = FOX_HEAD_DIM
    tk = tq // 2
    nq = seq // tq

    def head(offset):
        return pl.BlockSpec((1, seq, dh), lambda b, h: (b, 0, offset + h))

    return pl.pallas_call(
        functools.partial(_fox_kernel, tq=tq, sub=sub, seq=seq),
        grid=(batch, FOX_HEADS),
        in_specs=[head(0), head(FOX_HEADS), head(2 * FOX_HEADS),
                  pl.BlockSpec((1, seq, V7X_LANES), lambda b, h: (b, 0, 0))],
        out_specs=head(0),
        out_shape=jax.ShapeDtypeStruct((batch, seq, FOX_WIDTH), BF16),
        scratch_shapes=[pltpu.VMEM((seq // tk, 2 * dh, tk), BF16), pltpu.VMEM((seq, 2 * dh), BF16),
                        pltpu.VMEM((seq, 2 * dh), BF16), pltpu.VMEM((nq, tq, V7X_LANES), F32),
                        pltpu.VMEM((nq, tq, 2 * dh), F32), pltpu.VMEM((2, tq, tk), F32),
                        pltpu.VMEM((2, tq, tk), BF16), pltpu.VMEM((2, tq, V7X_LANES), F32)],
        compiler_params=_params("parallel", "parallel"),
        name="fox_attn",
    )(proj3, proj3, proj3, cols)


def _mlstm_kernel(q_ref, k_ref, v_ref, o_ref, gn_ref, rows_ref, cols_ref, w0_ref, w1_ref, w2_ref,
                  w3_ref, y_ref, w0b_ref, w1b_ref, w2b_ref, w3b_ref, c_ref, n_ref, m_ref, *, batch):
    j = pl.program_id(0)
    L = CHUNK
    dh = MLSTM_HEAD_DIM

    @pl.when(j == 0)
    def _():
        c_ref[...] = jnp.zeros_like(c_ref)
        n_ref[...] = jnp.zeros_like(n_ref)
        m_ref[...] = jnp.zeros_like(m_ref)

    for src, dst in ((w0_ref, w0b_ref), (w1_ref, w1b_ref), (w2_ref, w2b_ref), (w3_ref, w3b_ref)):
        dst[...] = src[...].astype(BF16)

    lanes = V7X_LANES
    assert L == lanes and dh % lanes == 0
    halves = [slice(c0, c0 + lanes) for c0 in range(0, dh, lanes)]
    r = lax.broadcasted_iota(jnp.int32, (L, L), 0)
    c = lax.broadcasted_iota(jnp.int32, (L, L), 1)
    causal = c <= r
    chains = [(b, h) for b in range(batch) for h in range(MLSTM_HEADS)]

    def rep(x):
        return jnp.broadcast_to(x, (L, lanes))

    sel_r = lax.broadcasted_iota(jnp.int32, (3 * lanes, 2 * lanes), 0) % lanes
    sel_c = lax.broadcasted_iota(jnp.int32, (3 * lanes, 2 * lanes), 1)
    pieces = []
    for b in range(batch):
        hi, mid, lo = _split3(cols_ref[b])
        pieces.append(jnp.concatenate([hi, mid, lo], axis=1).astype(BF16))

    st1 = []
    for b, h in chains:
        hs = slice(h * dh, (h + 1) * dh)
        st = b * MLSTM_HEADS + h
        sel = sel_r == jnp.where(sel_c < lanes, ROW_MLB + h, ROW_MLI + h)
        bi = _dot(pieces[b], jnp.where(sel, 1.0, 0.0).astype(BF16))
        bcol, icol = bi[:, 0:lanes], bi[:, lanes:2 * lanes]
        qb = q_ref[b, :, hs]
        kb = k_ref[b, :, hs]
        brow = rows_ref[b, ROW_MLB + h:ROW_MLB + h + 1, :]
        irow = rows_ref[b, ROW_MLI + h:ROW_MLI + h + 1, :]
        m_prev = m_ref[st]
        c_prev = c_ref[st]
        qk_raw = _dot_nt(qb, kb)
        qc = _dot(qb, c_prev.astype(BF16))

        dlog = jnp.where(causal, bcol - brow + irow, -jnp.inf)
        inter = bcol + m_prev
        m_t = jnp.maximum(inter, rep(jnp.max(dlog, axis=-1, keepdims=True)))
        w = jnp.exp(dlog - m_t)
        a = jnp.exp(inter - m_t)
        b_end = brow[:, L - 1:L]
        m_new = jnp.maximum(b_end + m_prev, jnp.max(b_end - brow + irow, axis=-1, keepdims=True))
        decay = jnp.exp(b_end + m_prev - m_new)
        ws = jnp.exp(b_end - bcol + icol - m_new)
        qn = rep(jnp.sum(qb.astype(F32) * n_ref[st], axis=-1, keepdims=True))
        st1.append((hs, st, kb, c_prev, qk_raw, qc, m_t, w, a, m_new, decay, ws, qn))

    st2 = []
    for (b, h), (hs, st, kb, c_prev, qk_raw, qc, m_t, w, a, m_new, decay, ws, qn) in zip(chains, st1):
        v = v_ref[b, :, hs]
        qk = qk_raw * w
        k = kb.astype(F32)
        kw = jnp.concatenate([k[:, cs] * ws for cs in halves], axis=1)
        pv = _dot(qk.astype(BF16), v)
        upd = _dot_tn(kw.astype(BF16), v)
        st2.append((qk, kw, pv, upd))

    for (b, h), s1, (qk, kw, pv, upd) in zip(chains, st1, st2):
        hs, st, kb, c_prev, qk_raw, qc, m_t, w, a, m_new, decay, ws, qn = s1
        den = a * qn + rep(jnp.sum(qk, axis=-1, keepdims=True))
        inv = 1.0 / jnp.maximum(jnp.abs(den), jnp.exp(-m_t))
        h_parts = [(a * qc[:, cs] + pv[:, cs]) * inv for cs in halves]
        sq = h_parts[0] * h_parts[0]
        for hp in h_parts[1:]:
            sq = sq + hp * hp
        rms = lax.rsqrt(rep(jnp.sum(sq, axis=-1, keepdims=True)) * (1.0 / dh) + EPS)
        for cs, hp in zip(halves, h_parts):
            oc = slice(hs.start + cs.start, hs.start + cs.stop)
            y = hp * rms * gn_ref[:, oc] * o_ref[b, :, oc].astype(F32)
            y_ref[b, :, oc] = y.astype(BF16)
        c_ref[st] = decay * c_prev + upd
        n_ref[st] = decay * n_ref[st] + jnp.sum(kw, axis=0, keepdims=True)
        m_ref[st] = m_new


def _mlstm(proj3, gn, rows, cols, weights):
    batch, seq, _ = proj3.shape
    dh = MLSTM_HEAD_DIM
    width = MLSTM_WIDTH
    nc = seq // CHUNK
    chains = batch * MLSTM_HEADS

    def col(base):
        return lambda j: (0, j, base // width)

    shares = [w.shape[0] // nc for w in weights]
    assert all(s * nc == w.shape[0] and s % 16 == 0 for s, w in zip(shares, weights))
    w_specs = [pl.BlockSpec((s, w.shape[1]), lambda j: (j, 0)) for s, w in zip(shares, weights)]

    return pl.pallas_call(
        functools.partial(_mlstm_kernel, batch=batch),
        grid=(nc,),
        in_specs=[
            pl.BlockSpec((batch, CHUNK, width), col(COL_LQ)),
            pl.BlockSpec((batch, CHUNK, width), col(COL_LK)),
            pl.BlockSpec((batch, CHUNK, width), col(COL_LV)),
            pl.BlockSpec((batch, CHUNK, width), col(COL_LO)),
            pl.BlockSpec((1, width), lambda j: (0, 0)),
            pl.BlockSpec((batch, N_SMALL, CHUNK), lambda j: (0, 0, j)),
            pl.BlockSpec((batch, CHUNK, V7X_LANES), lambda j: (0, j, 0)),
        ] + w_specs,
        out_specs=[pl.BlockSpec((batch, CHUNK, width), lambda j: (0, j, 0))] + w_specs,
        out_shape=[jax.ShapeDtypeStruct((batch, seq, width), BF16)]
        + [jax.ShapeDtypeStruct(w.shape, BF16) for w in weights],
        scratch_shapes=[pltpu.VMEM((chains, dh, dh), F32), pltpu.VMEM((chains, 1, dh), F32),
                        pltpu.VMEM((chains, 1, 1), F32)],
        compiler_params=_params("arbitrary"),
        name="mlstm",
    )(proj3, proj3, proj3, proj3, gn, rows, cols, *weights)


def _mem_kv_kernel(mem_ref, g_ref, w_ref, o_ref):
    u = _rms(mem_ref[...], g_ref[...]).astype(BF16)
    o_ref[...] = _dot(u, w_ref[...].astype(BF16)).astype(BF16)


def _mem_kv(mem2, g, w, bn=512):
    m = mem2.shape[0]
    n = w.shape[1]
    return pl.pallas_call(
        _mem_kv_kernel,
        grid=(n // bn,),
        in_specs=[
            pl.BlockSpec((m, D_MODEL), lambda j: (0, 0)),
            pl.BlockSpec((1, D_MODEL), lambda j: (0, 0)),
            pl.BlockSpec((D_MODEL, bn), lambda j: (0, j)),
        ],
        out_specs=pl.BlockSpec((m, bn), lambda j: (0, j)),
        out_shape=jax.ShapeDtypeStruct((m, n), BF16),
        compiler_params=_params("parallel"),
        name="mem_kv",
    )(mem2, g, w)


def _mem_attn_kernel(q_ref, k_ref, v_ref, o_ref):
    dh = MEM_HEAD_DIM
    for hh in range(MEM_HEADS):
        sl = slice(hh * dh, (hh + 1) * dh)
        s = _dot_nt(q_ref[0, :, sl], k_ref[0, :, sl])
        p = jnp.exp(s - jnp.max(s, axis=-1, keepdims=True))
        l = jnp.sum(p, axis=-1, keepdims=True)
        o_ref[0, :, sl] = (_dot(p.astype(BF16), v_ref[0, :, sl]) / l).astype(BF16)


def _mem_attn(proj3, mkv3, tq=512):
    batch, seq, _ = proj3.shape
    mem_len = mkv3.shape[1]
    return pl.pallas_call(
        _mem_attn_kernel,
        grid=(batch, seq // tq),
        in_specs=[
            pl.BlockSpec((1, tq, MEM_WIDTH), lambda b, i: (b, i, COL_MQ // MEM_WIDTH)),
            pl.BlockSpec((1, mem_len, MEM_WIDTH), lambda b, i: (b, 0, 0)),
            pl.BlockSpec((1, mem_len, MEM_WIDTH), lambda b, i: (b, 0, 1)),
        ],
        out_specs=pl.BlockSpec((1, tq, MEM_WIDTH), lambda b, i: (b, i, 0)),
        out_shape=jax.ShapeDtypeStruct((batch, seq, MEM_WIDTH), BF16),
        compiler_params=_params("parallel", "parallel"),
        name="mem_attn",
    )(proj3, mkv3, mkv3)


MERGE_ROWS = 256


def _merge_out_kernel(yf_ref, ym_ref, yc_ref, g0_ref, g1_ref, g2_ref, wf_ref, wm_ref, wc_ref, wo_ref,
                      x_ref, gn_ref, h_ref, u_ref):
    for r0 in range(0, x_ref.shape[0], MERGE_ROWS):
        rs = slice(r0, r0 + MERGE_ROWS)
        merged = g0_ref[rs, :].astype(F32) * _dot(yf_ref[rs, :], wf_ref[...])
        merged = merged + g1_ref[rs, :].astype(F32) * _dot(ym_ref[rs, :], wm_ref[...])
        merged = merged + g2_ref[rs, :].astype(F32) * _dot(yc_ref[rs, :], wc_ref[...])
        h = x_ref[rs, :] + _dot(merged.astype(BF16), wo_ref[...])
        h_ref[rs, :] = h
        u_ref[rs, :] = _rms(h, gn_ref[...]).astype(BF16)


def _merge_out(y_fox, y_ml, y_mem, proj, w_f, w_m, w_c, w_o, x2, g_ffn, bm=256):
    m = x2.shape[0]
    kdim = y_fox.shape[1]
    y_spec = pl.BlockSpec((bm, kdim), lambda i: (i, 0))
    row_spec = pl.BlockSpec((bm, D_MODEL), lambda i: (i, 0))

    def gate_spec(branch):
        return pl.BlockSpec((bm, D_MODEL), lambda i: (i, COL_GATES // D_MODEL + branch))

    def resident(rows):
        return pl.BlockSpec((rows, D_MODEL), lambda i: (0, 0), pipeline_mode=pl.Buffered(1))

    return pl.pallas_call(
        _merge_out_kernel,
        grid=(m // bm,),
        in_specs=[y_spec, y_spec, y_spec, gate_spec(0), gate_spec(1), gate_spec(2),
                  resident(kdim), resident(kdim), resident(kdim), resident(D_MODEL),
                  row_spec, pl.BlockSpec((1, D_MODEL), lambda i: (0, 0))],
        out_specs=[row_spec, row_spec],
        out_shape=[
            jax.ShapeDtypeStruct((m, D_MODEL), F32),
            jax.ShapeDtypeStruct((m, D_MODEL), BF16),
        ],
        compiler_params=_params("parallel"),
        name="merge_out",
    )(y_fox, y_ml, y_mem, proj, proj, proj, w_f, w_m, w_c, w_o, x2, g_ffn)


def _ffn_in_kernel(u_ref, wg_ref, wu_ref, wo_ref, o_ref, wob_ref, wgb_ref, wub_ref):
    @pl.when(pl.program_id(1) == 0)
    def _():
        _cast_weight(wgb_ref, lambda sl: wg_ref[sl, :], D_MODEL)
        _cast_weight(wub_ref, lambda sl: wu_ref[sl, :], D_MODEL)

    wob_ref[...] = wo_ref[...].astype(BF16)
    u = u_ref[...]
    gate = _dot(u, wgb_ref[...])
    up = _dot(u, wub_ref[...])
    o_ref[...] = (gate * jax.nn.sigmoid(gate) * up).astype(BF16)


def _ffn_in(u, w, w_out_f32, bm=1024, bn=512):
    m = u.shape[0]
    nb = D_FF // bn
    mt = m // bm
    share = D_FF // (nb * mt)
    assert share * nb * mt == D_FF and share % 16 == 0
    return pl.pallas_call(
        _ffn_in_kernel,
        grid=(nb, mt),
        in_specs=[
            pl.BlockSpec((bm, D_MODEL), lambda j, i: (i, 0)),
            pl.BlockSpec((D_MODEL, bn), lambda j, i: (0, j)),
            pl.BlockSpec((D_MODEL, bn), lambda j, i: (0, nb + j)),
            pl.BlockSpec((share, D_MODEL), lambda j, i: (j * mt + i, 0)),
        ],
        out_specs=[
            pl.BlockSpec((bm, bn), lambda j, i: (i, j)),
            pl.BlockSpec((share, D_MODEL), lambda j, i: (j * mt + i, 0)),
        ],
        out_shape=[
            jax.ShapeDtypeStruct((m, D_FF), BF16),
            jax.ShapeDtypeStruct((D_FF, D_MODEL), BF16),
        ],
        scratch_shapes=[pltpu.VMEM((D_MODEL, bn), BF16)] * 2,
        compiler_params=_params("arbitrary", "arbitrary"),
        name="ffn_in",
    )(u, w, w, w_out_f32)


def _ffn_out_kernel(a_ref, w_ref, r_ref, g_ref, o_ref, *, final_norm):
    h = r_ref[...] + _dot(a_ref[...], w_ref[...])
    o_ref[...] = _rms(h, g_ref[...]) if final_norm else h


def _ffn_out(act, w_bf16, resid, final_gain, bm=256):
    m = act.shape[0]
    final_norm = final_gain is not None
    gain = final_gain if final_norm else jnp.ones((1, D_MODEL), F32)
    return pl.pallas_call(
        functools.partial(_ffn_out_kernel, final_norm=final_norm),
        grid=(m // bm,),
        in_specs=[
            pl.BlockSpec((bm, D_FF), lambda i: (i, 0)),
            pl.BlockSpec((D_FF, D_MODEL), lambda i: (0, 0), pipeline_mode=pl.Buffered(1)),
            pl.BlockSpec((bm, D_MODEL), lambda i: (i, 0)),
            pl.BlockSpec((1, D_MODEL), lambda i: (0, 0)),
        ],
        out_specs=pl.BlockSpec((bm, D_MODEL), lambda i: (i, 0)),
        out_shape=jax.ShapeDtypeStruct((m, D_MODEL), F32),
        compiler_params=_params("parallel"),
        name="ffn_out",
    )(act, w_bf16, resid, gain)


def _layer(h2, mem2, batch, seq, norm_mix, w_in, b_in, conv_w, conv_b, mlstm_norm, norm_mem,
           w_mem_kv, w_br_fox, w_br_mlstm, w_br_mem, w_out, norm_ffn, w_ffn_in, w_ffn_out,
           final_gain):
    b_big = jnp.concatenate([b_in[:OFF_FF], b_in[OFF_LQ:OFF_LI], b_in[OFF_LO:]])[None, :]
    b_small_t = jnp.concatenate([b_in[OFF_FF:OFF_LQ], b_in[OFF_LI:OFF_LO]])[:, None]

    w_in_t = w_in.T
    u, gst = _norm_in(h2, norm_mix[None, :], w_in_t, b_small_t)
    proj = _in_proj(u, w_in_t, b_big, conv_w, conv_b[None, :], seq)
    rows, cols = _gates(gst, batch, seq)
    proj3 = proj.reshape(batch, seq, PROJ_BIG)

    y_fox = _fox(proj3, cols)
    y_ml, wf_b, wm_b, wc_b, wo_b = _mlstm(proj3, mlstm_norm[None, :], rows, cols,
                                          (w_br_fox, w_br_mlstm, w_br_mem, w_out))
    mkv = _mem_kv(mem2, norm_mem[None, :], w_mem_kv)
    y_mem = _mem_attn(proj3, mkv.reshape(batch, -1, 2 * MEM_WIDTH))

    tokens = batch * seq
    h2, u_ffn = _merge_out(y_fox.reshape(tokens, FOX_WIDTH), y_ml.reshape(tokens, MLSTM_WIDTH),
                           y_mem.reshape(tokens, MEM_WIDTH), proj, wf_b, wm_b, wc_b, wo_b,
                           h2, norm_ffn[None, :])
    act, w_ffn_out_bf16 = _ffn_in(u_ffn, w_ffn_in, w_ffn_out)
    return _ffn_out(act, w_ffn_out_bf16, h2, final_gain)


def kernel(x, mem, norm_mix, w_in, b_in, conv_w, conv_b, mlstm_norm, norm_mem, w_mem_kv, w_br_fox,
           w_br_mlstm, w_br_mem, w_out, norm_ffn, w_ffn_in, w_ffn_out, norm_final):
    batch, seq, d = x.shape
    assert d == D_MODEL and seq % CHUNK == 0
    h2 = x.reshape(batch * seq, d)
    mem2 = mem.reshape(batch * mem.shape[1], d)
    depth = norm_mix.shape[0]
    for l in range(depth):
        final_gain = norm_final[None, :] if l == depth - 1 else None
        h2 = _layer(h2, mem2, batch, seq, norm_mix[l], w_in[l], b_in[l], conv_w[l], conv_b[l],
                    mlstm_norm[l], norm_mem[l], w_mem_kv[l], w_br_fox[l], w_br_mlstm[l],
                    w_br_mem[l], w_out[l], norm_ffn[l], w_ffn_in[l], w_ffn_out[l], final_gain)
    return h2.reshape(batch, seq, d)
```

```python
import functools

import jax
import jax.numpy as jnp
from jax import lax
from jax.experimental import pallas as pl
from jax.experimental.pallas import tpu as pltpu

D_MODEL = 2048
FOX_HEADS = 8
FOX_HEAD_DIM = 128
FOX_WIDTH = FOX_HEADS * FOX_HEAD_DIM
MLSTM_HEADS = 4
MLSTM_HEAD_DIM = 256
MLSTM_WIDTH = MLSTM_HEADS * MLSTM_HEAD_DIM
MEM_HEADS = 4
MEM_HEAD_DIM = 256
MEM_WIDTH = MEM_HEADS * MEM_HEAD_DIM
N_BRANCH = 3
CONV_WIDTH = 4
CHUNK = 128
D_FF = 5632
EPS = 1e-6

OFF_FF = 3 * FOX_WIDTH
OFF_LQ = OFF_FF + FOX_HEADS
OFF_LI = OFF_LQ + 3 * MLSTM_WIDTH
OFF_LO = OFF_LI + 2 * MLSTM_HEADS
N_SMALL = FOX_HEADS + 2 * MLSTM_HEADS
PROJ_BIG = 3 * FOX_WIDTH + 3 * MLSTM_WIDTH + MLSTM_WIDTH + MEM_WIDTH + N_BRANCH * D_MODEL

COL_LQ = 3 * FOX_WIDTH
COL_LK = COL_LQ + MLSTM_WIDTH
COL_LV = COL_LK + MLSTM_WIDTH
COL_LO = COL_LV + MLSTM_WIDTH
COL_MQ = COL_LO + MLSTM_WIDTH
COL_GATES = COL_MQ + MEM_WIDTH

ROW_FOX = 0
ROW_MLI = FOX_HEADS
ROW_MLB = FOX_HEADS + MLSTM_HEADS

V7X_LANES = 128
V7X_VMEM_LIMIT = 56 * 1024 * 1024

F32 = jnp.float32
BF16 = jnp.bfloat16


def _params(*sem):
    return pltpu.CompilerParams(dimension_semantics=sem, vmem_limit_bytes=V7X_VMEM_LIMIT)


def _rms(x, g):
    return x * lax.rsqrt(jnp.mean(x * x, axis=-1, keepdims=True) + EPS) * g


def _dot(a, b):
    return jnp.dot(a, b, preferred_element_type=F32)


def _dot_nt(a, b):
    return lax.dot_general(a, b, (((1,), (1,)), ((), ())), preferred_element_type=F32)


def _dot_tn(a, b):
    return lax.dot_general(a, b, (((0,), (0,)), ((), ())), preferred_element_type=F32)


def _log_sigmoid(x):
    return jnp.minimum(x, 0.0) - jnp.log1p(jnp.exp(-jnp.abs(x)))


SUBLANES = 8


def _norm_in_kernel(x_ref, g_ref, wf_ref, wl_ref, bst_ref, u_ref, gst_ref, wt_ref):
    @pl.when(pl.program_id(0) == 0)
    def _():
        wt_ref[...] = jnp.concatenate([wf_ref[...], wl_ref[...]], axis=0).astype(BF16)

    u = _rms(x_ref[...], g_ref[...]).astype(BF16)
    u_ref[...] = u
    gst_ref[...] = _dot_nt(wt_ref[...], u) + bst_ref[...]


def _norm_in(x2, g, w_in_t, b_small_t, bm=512):
    m = x2.shape[0]
    assert FOX_HEADS == SUBLANES and 2 * MLSTM_HEADS == SUBLANES
    return pl.pallas_call(
        _norm_in_kernel,
        grid=(m // bm,),
        in_specs=[
            pl.BlockSpec((bm, D_MODEL), lambda i: (i, 0)),
            pl.BlockSpec((1, D_MODEL), lambda i: (0, 0)),
            pl.BlockSpec((SUBLANES, D_MODEL), lambda i: (OFF_FF // SUBLANES, 0)),
            pl.BlockSpec((SUBLANES, D_MODEL), lambda i: (OFF_LI // SUBLANES, 0)),
            pl.BlockSpec((N_SMALL, 1), lambda i: (0, 0)),
        ],
        out_specs=[
            pl.BlockSpec((bm, D_MODEL), lambda i: (i, 0)),
            pl.BlockSpec((N_SMALL, bm), lambda i: (0, i)),
        ],
        out_shape=[
            jax.ShapeDtypeStruct((m, D_MODEL), BF16),
            jax.ShapeDtypeStruct((N_SMALL, m), F32),
        ],
        scratch_shapes=[pltpu.VMEM((N_SMALL, D_MODEL), BF16)],
        compiler_params=_params("arbitrary"),
        name="norm_in",
    )(x2, g, w_in_t, w_in_t, b_small_t)


CAST_ROWS = 256
EPILOGUE_ROWS = 256


def _cast_weight(dst_ref, src_fn, rows):
    def chunk(c, carry):
        sl = pl.ds(pl.multiple_of(c * CAST_ROWS, CAST_ROWS), CAST_ROWS)
        dst_ref[sl, :] = src_fn(sl).astype(BF16)
        return carry

    lax.fori_loop(0, rows // CAST_ROWS, chunk, 0)


def _in_proj_kernel(u_ref, w_ref, wx_ref, b_ref, cw_ref, cb_ref, o_ref, wb_ref, xs_ref,
                    *, bn, tiles_per_seq):
    j = pl.program_id(0)
    first_lq = OFF_FF // bn
    first_lo = (OFF_LI - FOX_HEADS) // bn

    def cast_shifted(shift):
        def body():
            def chunk(c, carry):
                src = pl.ds(pl.multiple_of(c * CAST_ROWS + shift, SUBLANES), CAST_ROWS)
                dst = pl.ds(pl.multiple_of(c * CAST_ROWS, CAST_ROWS), CAST_ROWS)
                wb_ref[dst, :] = w_ref[src, :].astype(BF16)
                return carry

            lax.fori_loop(0, bn // CAST_ROWS - 1, chunk, 0)
            last = bn - CAST_ROWS
            tail = jnp.concatenate([w_ref[last + shift:bn, :], wx_ref[0:shift, :]], axis=0)
            wb_ref[last:bn, :] = tail.astype(BF16)
        return body

    @pl.when(pl.program_id(1) == 0)
    def _():
        @pl.when(j < first_lq)
        def _():
            _cast_weight(wb_ref, lambda sl: w_ref[sl, :], bn)

        pl.when(jnp.logical_and(j >= first_lq, j < first_lo))(cast_shifted(FOX_HEADS))
        pl.when(j >= first_lo)(cast_shifted(N_SMALL))

    i = pl.program_id(1)
    bm = u_ref.shape[0]
    t_lq, t_lk, t_lo, t_mq = (c // bn for c in (COL_LQ, COL_LK, COL_LO, COL_MQ))
    is_conv = jnp.logical_or(j == t_lq, j == t_lk)
    is_gate = jnp.logical_or(j == t_lo, j > t_mq)

    row_tiles = [slice(r0, r0 + EPILOGUE_ROWS) for r0 in range(0, bm, EPILOGUE_ROWS)]

    def raw(rs):
        return _dot_nt(u_ref[rs, :], wb_ref[...]) + b_ref[...]

    @pl.when(jnp.logical_not(jnp.logical_or(is_conv, is_gate)))
    def _():
        mult = jnp.where(j == 0, FOX_Q_SCALE, jnp.where(j == t_mq, MEM_HEAD_DIM ** -0.5, 1.0))
        for rs in row_tiles:
            o_ref[rs, :] = (raw(rs) * mult).astype(BF16)

    @pl.when(is_gate)
    def _():
        for rs in row_tiles:
            o_ref[rs, :] = jax.nn.sigmoid(raw(rs)).astype(BF16)

    @pl.when(is_conv)
    def _():
        cw = cw_ref[...]
        mult = jnp.where(j == t_lk, MLSTM_HEAD_DIM ** -0.5, 1.0)
        @pl.when(i % tiles_per_seq == 0)
        def _():
            xs_ref[0, 0:SUBLANES, :] = jnp.zeros((SUBLANES, bn), F32)

        for n, rs in enumerate(row_tiles):
            cur, nxt = n % 2, (n + 1) % 2
            x = raw(rs)
            xs_ref[cur, SUBLANES:SUBLANES + EPILOGUE_ROWS, :] = x
            xs_ref[nxt, 0:SUBLANES, :] = x[EPILOGUE_ROWS - SUBLANES:EPILOGUE_ROWS, :]
            y = cb_ref[...] + cw[CONV_WIDTH - 1:CONV_WIDTH, :] * x
            for t in range(1, CONV_WIDTH):
                y = y + (cw[CONV_WIDTH - 1 - t:CONV_WIDTH - t, :]
                         * xs_ref[cur, SUBLANES - t:SUBLANES - t + EPILOGUE_ROWS, :])
            o_ref[rs, :] = (y * jax.nn.sigmoid(y) * mult).astype(BF16)


def _in_proj(u, w_in_t, b_big, conv_w, conv_b, seq, bm=2048, bn=FOX_WIDTH):
    m = u.shape[0]
    assert bn == FOX_WIDTH and OFF_FF % bn == 0 and (OFF_LI - FOX_HEADS) % bn == 0
    assert seq % bm == 0 and MLSTM_WIDTH == bn and CONV_WIDTH - 1 <= SUBLANES
    extra_per_tile = bn // N_SMALL
    t_lq = COL_LQ // bn
    return pl.pallas_call(
        functools.partial(_in_proj_kernel, bn=bn, tiles_per_seq=seq // bm),
        grid=(PROJ_BIG // bn, m // bm),
        in_specs=[
            pl.BlockSpec((bm, D_MODEL), lambda j, i: (i, 0)),
            pl.BlockSpec((bn, D_MODEL), lambda j, i: (j, 0)),
            pl.BlockSpec((N_SMALL, D_MODEL), lambda j, i: ((j + 1) * extra_per_tile, 0)),
            pl.BlockSpec((1, bn), lambda j, i: (0, j)),
            pl.BlockSpec((CONV_WIDTH, bn), lambda j, i: (0, jnp.clip(j - t_lq, 0, 1))),
            pl.BlockSpec((1, bn), lambda j, i: (0, jnp.clip(j - t_lq, 0, 1))),
        ],
        out_specs=pl.BlockSpec((bm, bn), lambda j, i: (i, j)),
        out_shape=jax.ShapeDtypeStruct((m, PROJ_BIG), BF16),
        scratch_shapes=[pltpu.VMEM((bn, D_MODEL), BF16),
                        pltpu.VMEM((2, SUBLANES + EPILOGUE_ROWS, bn), F32)],
        compiler_params=_params("arbitrary", "arbitrary"),
        name="in_proj",
    )(u, w_in_t, w_in_t, b_big, conv_w, conv_b)


def _gates_kernel(gst_ref, rows_ref, cols_ref, *, seq):
    g = gst_ref[...]
    row = lax.broadcasted_iota(jnp.int32, g.shape, 0)
    lane = lax.broadcasted_iota(jnp.int32, g.shape, 1)
    in_chunk = lane % CHUNK
    is_input_gate = jnp.logical_and(row >= ROW_MLI, row < ROW_MLB)
    local = jnp.where(is_input_gate, g, _log_sigmoid(g))
    d = 1
    while d < CHUNK:
        local = local + jnp.where(in_chunk >= d, pltpu.roll(local, d, axis=1), 0.0)
        d *= 2
    carry = pltpu.roll(jnp.where(in_chunk == CHUNK - 1, local, 0.0), 1, axis=1)
    carry = jnp.where(lane == 0, 0.0, carry)
    d = 1
    while d < CHUNK:
        carry = carry + jnp.where(in_chunk >= d, pltpu.roll(carry, d, axis=1), 0.0)
        d *= 2
    while d < seq:
        carry = carry + jnp.concatenate([jnp.zeros((N_SMALL, d), F32), carry[:, :seq - d]], axis=1)
        d *= 2
    out = jnp.where(row < ROW_MLI, local + carry, jnp.where(is_input_gate, g, local))
    rows_ref[0] = out
    padded = jnp.concatenate([out, jnp.zeros((V7X_LANES - N_SMALL, seq), F32)], axis=0)
    cols_ref[0] = padded.T


def _gates(gst, batch, seq):
    return pl.pallas_call(
        functools.partial(_gates_kernel, seq=seq),
        grid=(batch,),
        in_specs=[pl.BlockSpec((N_SMALL, seq), lambda b: (0, b))],
        out_specs=[
            pl.BlockSpec((1, N_SMALL, seq), lambda b: (b, 0, 0)),
            pl.BlockSpec((1, seq, V7X_LANES), lambda b: (b, 0, 0)),
        ],
        out_shape=[
            jax.ShapeDtypeStruct((batch, N_SMALL, seq), F32),
            jax.ShapeDtypeStruct((batch, seq, V7X_LANES), F32),
        ],
        compiler_params=_params("parallel"),
        name="gates",
    )(gst)


FOX_TQ = 1024
FOX_SUB = 256
LOG2E = 1.4426950408889634
FOX_Q_SCALE = FOX_HEAD_DIM ** -0.5 * LOG2E


def _split3(f):
    hi = f.astype(BF16).astype(F32)
    r = f - hi
    mid = r.astype(BF16).astype(F32)
    return hi, mid, r - mid


def _fox_kernel(q_ref, k_ref, v_ref, fcol_ref, o_ref, kaug_ref, vaug_ref, q2_ref, m_ref, acc_ref,
                s_ref, p_ref, alpha_ref, *, tq, sub, seq):
    h = pl.program_id(1)
    dh = FOX_HEAD_DIM
    tk = tq // 2
    nq = seq // tq

    def f_column(rows):
        lane = lax.broadcasted_iota(jnp.int32, (rows.shape[0], V7X_LANES), 1)
        return jnp.sum(jnp.where(lane == h, rows, 0.0), axis=-1, keepdims=True) * LOG2E

    for c in range(seq // tk):
        sl = slice(c * tk, (c + 1) * tk)
        hi, mid, lo = _split3(f_column(fcol_ref[0, sl, :]))
        lane = lax.broadcasted_iota(jnp.int32, (tk, V7X_LANES), 1)
        kaug = jnp.where(lane < 3, 1.0, jnp.where(lane == 3, -hi, jnp.where(lane == 4, -mid,
                         jnp.where(lane == 5, -lo, 0.0))))
        qaug = jnp.where(lane == 0, hi, jnp.where(lane == 1, mid, jnp.where(lane == 2, lo,
                         jnp.where(lane < 6, 1.0, 0.0))))
        k2 = jnp.concatenate([k_ref[0, sl, :].astype(F32), kaug], axis=1)
        kaug_ref[c] = k2.T.astype(BF16)
        vaug_ref[sl, 0:dh] = v_ref[0, sl, :]
        vaug_ref[sl, dh:2 * dh] = jnp.where(lane == 0, 1.0, 0.0).astype(BF16)
        q2_ref[sl, 0:dh] = q_ref[0, sl, :]
        q2_ref[sl, dh:2 * dh] = qaug.astype(BF16)

    m_ref[...] = jnp.full_like(m_ref, -jnp.inf)
    acc_ref[...] = jnp.zeros_like(acc_ref)

    all_subs = [slice(r0, r0 + sub) for r0 in range(0, tq, sub)]
    low_subs = [rs for rs in all_subs if rs.start >= tk]
    steps = []
    for qi in range(nq):
        steps += [(qi, t, None, all_subs) for t in range(2 * qi)]
        steps += [(qi, 2 * qi, 0, all_subs), (qi, 2 * qi + 1, tk, low_subs)]

    def logits(n):
        qi, t, _, subs = steps[n]
        kt = kaug_ref[t]
        for rs in subs:
            s_ref[n % 2, rs, :] = _dot(q2_ref[qi * tq + rs.start:qi * tq + rs.stop, :], kt)

    def softmax(n):
        qi, _, col0, subs = steps[n]
        slot = n % 2
        for rs in subs:
            s = s_ref[slot, rs, :]
            if col0 is not None and rs.start < col0 + tk:
                r = lax.broadcasted_iota(jnp.int32, (sub, tk), 0) + rs.start
                c = lax.broadcasted_iota(jnp.int32, (sub, tk), 1) + col0
                s = jnp.where(c <= r, s, -jnp.inf)
            m_prev = m_ref[qi, rs, :]
            m_new = jnp.maximum(m_prev, jnp.broadcast_to(jnp.max(s, axis=-1, keepdims=True), m_prev.shape))
            alpha_ref[slot, rs, :] = jnp.exp2(m_prev - m_new)
            for c0 in range(0, tk, V7X_LANES):
                cs = slice(c0, c0 + V7X_LANES)
                p_ref[slot, rs, cs] = jnp.exp2(s[:, cs] - m_new).astype(BF16)
            m_ref[qi, rs, :] = m_new

    def values(n):
        qi, t, _, subs = steps[n]
        slot = n % 2
        va = vaug_ref[t * tk:(t + 1) * tk, :]
        for rs in subs:
            pv = _dot(p_ref[slot, rs, :], va)
            alpha = alpha_ref[slot, rs, :]
            for c0 in range(0, 2 * dh, V7X_LANES):
                cs = slice(c0, c0 + V7X_LANES)
                acc_ref[qi, rs, cs] = alpha * acc_ref[qi, rs, cs] + pv[:, cs]
        if n + 1 == len(steps) or steps[n + 1][0] != qi:
            o_ref[0, qi * tq:(qi + 1) * tq, :] = (acc_ref[qi, :, 0:dh]
                                                  / acc_ref[qi, :, dh:dh + 1]).astype(BF16)

    logits(0)
    for n in range(len(steps)):
        if n > 0:
            values(n - 1)
        softmax(n)
        if n + 1 < len(steps):
            logits(n + 1)
    values(len(steps) - 1)


def _fox(proj3, cols, tq=FOX_TQ, sub=FOX_SUB):
    batch, seq, _ = proj3.shape
    dh = FOX_HEAD_DIM
    tk = tq // 2
    nq = seq // tq

    def head(offset):
        return pl.BlockSpec((1, seq, dh), lambda b, h: (b, 0, offset + h))

    return pl.pallas_call(
        functools.partial(_fox_kernel, tq=tq, sub=sub, seq=seq),
        grid=(batch, FOX_HEADS),
        in_specs=[head(0), head(FOX_HEADS), head(2 * FOX_HEADS),
                  pl.BlockSpec((1, seq, V7X_LANES), lambda b, h: (b, 0, 0))],
        out_specs=head(0),
        out_shape=jax.ShapeDtypeStruct((batch, seq, FOX_WIDTH), BF16),
        scratch_shapes=[pltpu.VMEM((seq // tk, 2 * dh, tk), BF16), pltpu.VMEM((seq, 2 * dh), BF16),
                        pltpu.VMEM((seq, 2 * dh), BF16), pltpu.VMEM((nq, tq, V7X_LANES), F32),
                        pltpu.VMEM((nq, tq, 2 * dh), F32), pltpu.VMEM((2, tq, tk), F32),
                        pltpu.VMEM((2, tq, tk), BF16), pltpu.VMEM((2, tq, V7X_LANES), F32)],
        compiler_params=_params("parallel", "parallel"),
        name="fox_attn",
    )(proj3, proj3, proj3, cols)


MLSTM_GROUP = 2


def _mlstm_kernel(q_ref, k_ref, v_ref, o_ref, gn_ref, rows_ref, cols_ref, w0_ref, w1_ref, w2_ref,
                  w3_ref, y_ref, w0b_ref, w1b_ref, w2b_ref, w3b_ref, c_ref, n_ref, m_ref, *, batch):
    j = pl.program_id(0)
    L = CHUNK
    dh = MLSTM_HEAD_DIM

    @pl.when(j == 0)
    def _():
        c_ref[...] = jnp.zeros_like(c_ref)
        n_ref[...] = jnp.zeros_like(n_ref)
        m_ref[...] = jnp.zeros_like(m_ref)

    for src, dst in ((w0_ref, w0b_ref), (w1_ref, w1b_ref), (w2_ref, w2b_ref), (w3_ref, w3b_ref)):
        dst[...] = src[...].astype(BF16)

    lanes = V7X_LANES
    assert L == lanes and dh % lanes == 0
    halves = [slice(c0, c0 + lanes) for c0 in range(0, dh, lanes)]
    r = lax.broadcasted_iota(jnp.int32, (L, L), 0)
    c = lax.broadcasted_iota(jnp.int32, (L, L), 1)
    causal = c <= r
    all_chains = [(b, h) for b in range(batch) for h in range(MLSTM_HEADS)]

    def rep(x):
        return jnp.broadcast_to(x, (L, lanes))

    sel_r = lax.broadcasted_iota(jnp.int32, (3 * lanes, 2 * lanes), 0) % lanes
    sel_c = lax.broadcasted_iota(jnp.int32, (3 * lanes, 2 * lanes), 1)
    pieces = []
    for b in range(batch):
        hi, mid, lo = _split3(cols_ref[b])
        pieces.append(jnp.concatenate([hi, mid, lo], axis=1).astype(BF16))

    def run_group(chains):
        st1 = []
        for b, h in chains:
            hs = slice(h * dh, (h + 1) * dh)
            st = b * MLSTM_HEADS + h
            sel = sel_r == jnp.where(sel_c < lanes, ROW_MLB + h, ROW_MLI + h)
            bi = _dot(pieces[b], jnp.where(sel, 1.0, 0.0).astype(BF16))
            bcol, icol = bi[:, 0:lanes], bi[:, lanes:2 * lanes]
            qb = q_ref[b, :, hs]
            kb = k_ref[b, :, hs]
            brow = rows_ref[b, ROW_MLB + h:ROW_MLB + h + 1, :]
            irow = rows_ref[b, ROW_MLI + h:ROW_MLI + h + 1, :]
            m_prev = m_ref[st]
            c_prev = c_ref[st]
            qk_raw = _dot_nt(qb, kb)
            qc = _dot(qb, c_prev.astype(BF16))

            dlog = jnp.where(causal, bcol - brow + irow, -jnp.inf)
            inter = bcol + m_prev
            m_t = jnp.maximum(inter, rep(jnp.max(dlog, axis=-1, keepdims=True)))
            w = jnp.exp(dlog - m_t)
            a = jnp.exp(inter - m_t)
            b_end = brow[:, L - 1:L]
            m_new = jnp.maximum(b_end + m_prev, jnp.max(b_end - brow + irow, axis=-1, keepdims=True))
            decay = jnp.exp(b_end + m_prev - m_new)
            ws = jnp.exp(b_end - bcol + icol - m_new)
            qn = rep(jnp.sum(qb.astype(F32) * n_ref[st], axis=-1, keepdims=True))
            st1.append((hs, st, kb, c_prev, qk_raw, qc, m_t, w, a, m_new, decay, ws, qn))

        st2 = []
        for (b, h), (hs, st, kb, c_prev, qk_raw, qc, m_t, w, a, m_new, decay, ws, qn) in zip(chains, st1):
            v = v_ref[b, :, hs]
            qk = qk_raw * w
            k = kb.astype(F32)
            kw = jnp.concatenate([k[:, cs] * ws for cs in halves], axis=1)
            pv = _dot(qk.astype(BF16), v)
            upd = _dot_tn(kw.astype(BF16), v)
            st2.append((qk, kw, pv, upd))

        for (b, h), s1, (qk, kw, pv, upd) in zip(chains, st1, st2):
            hs, st, kb, c_prev, qk_raw, qc, m_t, w, a, m_new, decay, ws, qn = s1
            den = a * qn + rep(jnp.sum(qk, axis=-1, keepdims=True))
            inv = 1.0 / jnp.maximum(jnp.abs(den), jnp.exp(-m_t))
            h_parts = [(a * qc[:, cs] + pv[:, cs]) * inv for cs in halves]
            sq = h_parts[0] * h_parts[0]
            for hp in h_parts[1:]:
                sq = sq + hp * hp
            rms = lax.rsqrt(rep(jnp.sum(sq, axis=-1, keepdims=True)) * (1.0 / dh) + EPS)
            for cs, hp in zip(halves, h_parts):
                oc = slice(hs.start + cs.start, hs.start + cs.stop)
                y = hp * rms * gn_ref[:, oc] * o_ref[b, :, oc].astype(F32)
                y_ref[b, :, oc] = y.astype(BF16)
            c_ref[st] = decay * c_prev + upd
            n_ref[st] = decay * n_ref[st] + jnp.sum(kw, axis=0, keepdims=True)
            m_ref[st] = m_new

    for g0 in range(0, len(all_chains), MLSTM_GROUP):
        run_group(all_chains[g0:g0 + MLSTM_GROUP])


def _mlstm(proj3, gn, rows, cols, weights):
    batch, seq, _ = proj3.shape
    dh = MLSTM_HEAD_DIM
    width = MLSTM_WIDTH
    nc = seq // CHUNK
    chains = batch * MLSTM_HEADS

    def col(base):
        return lambda j: (0, j, base // width)

    shares = [w.shape[0] // nc for w in weights]
    assert all(s * nc == w.shape[0] and s % 16 == 0 for s, w in zip(shares, weights))
    w_specs = [pl.BlockSpec((s, w.shape[1]), lambda j: (j, 0)) for s, w in zip(shares, weights)]

    return pl.pallas_call(
        functools.partial(_mlstm_kernel, batch=batch),
        grid=(nc,),
        in_specs=[
            pl.BlockSpec((batch, CHUNK, width), col(COL_LQ)),
            pl.BlockSpec((batch, CHUNK, width), col(COL_LK)),
            pl.BlockSpec((batch, CHUNK, width), col(COL_LV)),
            pl.BlockSpec((batch, CHUNK, width), col(COL_LO)),
            pl.BlockSpec((1, width), lambda j: (0, 0)),
            pl.BlockSpec((batch, N_SMALL, CHUNK), lambda j: (0, 0, j)),
            pl.BlockSpec((batch, CHUNK, V7X_LANES), lambda j: (0, j, 0)),
        ] + w_specs,
        out_specs=[pl.BlockSpec((batch, CHUNK, width), lambda j: (0, j, 0))] + w_specs,
        out_shape=[jax.ShapeDtypeStruct((batch, seq, width), BF16)]
        + [jax.ShapeDtypeStruct(w.shape, BF16) for w in weights],
        scratch_shapes=[pltpu.VMEM((chains, dh, dh), F32), pltpu.VMEM((chains, 1, dh), F32),
                        pltpu.VMEM((chains, 1, 1), F32)],
        compiler_params=_params("arbitrary"),
        name="mlstm",
    )(proj3, proj3, proj3, proj3, gn, rows, cols, *weights)


def _mem_kv_kernel(mem_ref, g_ref, w_ref, o_ref):
    u = _rms(mem_ref[...], g_ref[...]).astype(BF16)
    o_ref[...] = _dot(u, w_ref[...].astype(BF16)).astype(BF16)


def _mem_kv(mem2, g, w, bn=512):
    m = mem2.shape[0]
    n = w.shape[1]
    return pl.pallas_call(
        _mem_kv_kernel,
        grid=(n // bn,),
        in_specs=[
            pl.BlockSpec((m, D_MODEL), lambda j: (0, 0)),
            pl.BlockSpec((1, D_MODEL), lambda j: (0, 0)),
            pl.BlockSpec((D_MODEL, bn), lambda j: (0, j)),
        ],
        out_specs=pl.BlockSpec((m, bn), lambda j: (0, j)),
        out_shape=jax.ShapeDtypeStruct((m, n), BF16),
        compiler_params=_params("parallel"),
        name="mem_kv",
    )(mem2, g, w)


def _merge_out_kernel(yf_ref, ym_ref, mq_ref, mk_ref, mv_ref, g0_ref, g1_ref, g2_ref, wf_ref, wm_ref,
                      wc_ref, wo_ref, x_ref, gn_ref, h_ref, u_ref):
    dh = MEM_HEAD_DIM
    heads = [slice(hh * dh, (hh + 1) * dh) for hh in range(MEM_HEADS)]
    logits = [_dot_nt(mq_ref[:, sl], mk_ref[0, :, sl]) for sl in heads]
    merged = g0_ref[...].astype(F32) * _dot(yf_ref[...], wf_ref[...])
    y_mem = []
    for sl, s in zip(heads, logits):
        p = jnp.exp(s - jnp.max(s, axis=-1, keepdims=True))
        l = jnp.sum(p, axis=-1, keepdims=True)
        y_mem.append((_dot(p.astype(BF16), mv_ref[0, :, sl]) / l).astype(BF16))
    merged = merged + g1_ref[...].astype(F32) * _dot(ym_ref[...], wm_ref[...])
    merged = merged + g2_ref[...].astype(F32) * _dot(jnp.concatenate(y_mem, axis=1), wc_ref[...])
    h = x_ref[...] + _dot(merged.astype(BF16), wo_ref[...])
    h_ref[...] = h
    u_ref[...] = _rms(h, gn_ref[...]).astype(BF16)


def _merge_out(y_fox, y_ml, proj, mkv3, w_f, w_m, w_c, w_o, x2, g_ffn, seq, bm=256):
    m = x2.shape[0]
    kdim = y_fox.shape[1]
    mem_len = mkv3.shape[1]
    assert seq % bm == 0
    tiles_per_seq = seq // bm
    y_spec = pl.BlockSpec((bm, kdim), lambda i: (i, 0))
    row_spec = pl.BlockSpec((bm, D_MODEL), lambda i: (i, 0))

    def gate_spec(branch):
        return pl.BlockSpec((bm, D_MODEL), lambda i: (i, COL_GATES // D_MODEL + branch))

    def resident(rows):
        return pl.BlockSpec((rows, D_MODEL), lambda i: (0, 0), pipeline_mode=pl.Buffered(1))

    def mem_spec(half):
        return pl.BlockSpec((1, mem_len, MEM_WIDTH), lambda i: (i // tiles_per_seq, 0, half))

    return pl.pallas_call(
        _merge_out_kernel,
        grid=(m // bm,),
        in_specs=[y_spec, y_spec,
                  pl.BlockSpec((bm, MEM_WIDTH), lambda i: (i, COL_MQ // MEM_WIDTH)),
                  mem_spec(0), mem_spec(1),
                  gate_spec(0), gate_spec(1), gate_spec(2),
                  resident(kdim), resident(kdim), resident(kdim), resident(D_MODEL),
                  row_spec, pl.BlockSpec((1, D_MODEL), lambda i: (0, 0))],
        out_specs=[row_spec, row_spec],
        out_shape=[
            jax.ShapeDtypeStruct((m, D_MODEL), F32),
            jax.ShapeDtypeStruct((m, D_MODEL), BF16),
        ],
        compiler_params=_params("parallel"),
        name="merge_out",
    )(y_fox, y_ml, proj, mkv3, mkv3, proj, proj, proj, w_f, w_m, w_c, w_o, x2, g_ffn)


def _ffn_in_kernel(u_ref, wg_ref, wu_ref, wo_ref, o_ref, wob_ref, wgb_ref, wub_ref):
    @pl.when(pl.program_id(1) == 0)
    def _():
        _cast_weight(wgb_ref, lambda sl: wg_ref[sl, :], D_MODEL)
        _cast_weight(wub_ref, lambda sl: wu_ref[sl, :], D_MODEL)

    wob_ref[...] = wo_ref[...].astype(BF16)
    for r0 in range(0, u_ref.shape[0], EPILOGUE_ROWS):
        rs = slice(r0, r0 + EPILOGUE_ROWS)
        u = u_ref[rs, :]
        gate = _dot(u, wgb_ref[...])
        up = _dot(u, wub_ref[...])
        o_ref[rs, :] = (gate * jax.nn.sigmoid(gate) * up).astype(BF16)


def _ffn_in(u, w, w_out_f32, bm=1024, bn=512):
    m = u.shape[0]
    nb = D_FF // bn
    mt = m // bm
    share = D_FF // (nb * mt)
    assert share * nb * mt == D_FF and share % 16 == 0
    return pl.pallas_call(
        _ffn_in_kernel,
        grid=(nb, mt),
        in_specs=[
            pl.BlockSpec((bm, D_MODEL), lambda j, i: (i, 0)),
            pl.BlockSpec((D_MODEL, bn), lambda j, i: (0, j)),
            pl.BlockSpec((D_MODEL, bn), lambda j, i: (0, nb + j)),
            pl.BlockSpec((share, D_MODEL), lambda j, i: (j * mt + i, 0)),
        ],
        out_specs=[
            pl.BlockSpec((bm, bn), lambda j, i: (i, j)),
            pl.BlockSpec((share, D_MODEL), lambda j, i: (j * mt + i, 0)),
        ],
        out_shape=[
            jax.ShapeDtypeStruct((m, D_FF), BF16),
            jax.ShapeDtypeStruct((D_FF, D_MODEL), BF16),
        ],
        scratch_shapes=[pltpu.VMEM((D_MODEL, bn), BF16)] * 2,
        compiler_params=_params("arbitrary", "arbitrary"),
        name="ffn_in",
    )(u, w, w, w_out_f32)


def _ffn_out_kernel(a_ref, w_ref, r_ref, g_ref, o_ref, *, final_norm):
    h = r_ref[...] + _dot(a_ref[...], w_ref[...])
    o_ref[...] = _rms(h, g_ref[...]) if final_norm else h


def _ffn_out(act, w_bf16, resid, final_gain, bm=256):
    m = act.shape[0]
    final_norm = final_gain is not None
    gain = final_gain if final_norm else jnp.ones((1, D_MODEL), F32)
    return pl.pallas_call(
        functools.partial(_ffn_out_kernel, final_norm=final_norm),
        grid=(m // bm,),
        in_specs=[
            pl.BlockSpec((bm, D_FF), lambda i: (i, 0)),
            pl.BlockSpec((D_FF, D_MODEL), lambda i: (0, 0), pipeline_mode=pl.Buffered(1)),
            pl.BlockSpec((bm, D_MODEL), lambda i: (i, 0)),
            pl.BlockSpec((1, D_MODEL), lambda i: (0, 0)),
        ],
        out_specs=pl.BlockSpec((bm, D_MODEL), lambda i: (i, 0)),
        out_shape=jax.ShapeDtypeStruct((m, D_MODEL), F32),
        compiler_params=_params("parallel"),
        name="ffn_out",
    )(act, w_bf16, resid, gain)


def _layer(h2, mem2, batch, seq, norm_mix, w_in, b_in, conv_w, conv_b, mlstm_norm, norm_mem,
           w_mem_kv, w_br_fox, w_br_mlstm, w_br_mem, w_out, norm_ffn, w_ffn_in, w_ffn_out,
           final_gain):
    b_big = jnp.concatenate([b_in[:OFF_FF], b_in[OFF_LQ:OFF_LI], b_in[OFF_LO:]])[None, :]
    b_small_t = jnp.concatenate([b_in[OFF_FF:OFF_LQ], b_in[OFF_LI:OFF_LO]])[:, None]

    w_in_t = w_in.T
    u, gst = _norm_in(h2, norm_mix[None, :], w_in_t, b_small_t)
    proj = _in_proj(u, w_in_t, b_big, conv_w, conv_b[None, :], seq)
    rows, cols = _gates(gst, batch, seq)
    proj3 = proj.reshape(batch, seq, PROJ_BIG)

    y_fox = _fox(proj3, cols)
    y_ml, wf_b, wm_b, wc_b, wo_b = _mlstm(proj3, mlstm_norm[None, :], rows, cols,
                                          (w_br_fox, w_br_mlstm, w_br_mem, w_out))
    mkv = _mem_kv(mem2, norm_mem[None, :], w_mem_kv)

    tokens = batch * seq
    h2, u_ffn = _merge_out(y_fox.reshape(tokens, FOX_WIDTH), y_ml.reshape(tokens, MLSTM_WIDTH),
                           proj, mkv.reshape(batch, -1, 2 * MEM_WIDTH), wf_b, wm_b, wc_b, wo_b,
                           h2, norm_ffn[None, :], seq)
    act, w_ffn_out_bf16 = _ffn_in(u_ffn, w_ffn_in, w_ffn_out)
    return _ffn_out(act, w_ffn_out_bf16, h2, final_gain)


def kernel(x, mem, norm_mix, w_in, b_in, conv_w, conv_b, mlstm_norm, norm_mem, w_mem_kv, w_br_fox,
           w_br_mlstm, w_br_mem, w_out, norm_ffn, w_ffn_in, w_ffn_out, norm_final):
    batch, seq, d = x.shape
    assert d == D_MODEL and seq % CHUNK == 0
    h2 = x.reshape(batch * seq, d)
    mem2 = mem.reshape(batch * mem.shape[1], d)
    depth = norm_mix.shape[0]
    for l in range(depth):
        final_gain = norm_final[None, :] if l == depth - 1 else None
        h2 = _layer(h2, mem2, batch, seq, norm_mix[l], w_in[l], b_in[l], conv_w[l], conv_b[l],
                    mlstm_norm[l], norm_mem[l], w_mem_kv[l], w_br_fox[l], w_br_mlstm[l],
                    w_br_mem[l], w_out[l], norm_ffn[l], w_ffn_in[l], w_ffn_out[l], final_gain)
    return h2.reshape(batch, seq, d)
```

```python
import functools

import jax
import jax.numpy as jnp
from jax import lax
from jax.experimental import pallas as pl
from jax.experimental.pallas import tpu as pltpu

D_MODEL = 2048
FOX_HEADS = 8
FOX_HEAD_DIM = 128
FOX_WIDTH = FOX_HEADS * FOX_HEAD_DIM
MLSTM_HEADS = 4
MLSTM_HEAD_DIM = 256
MLSTM_WIDTH = MLSTM_HEADS * MLSTM_HEAD_DIM
MEM_HEADS = 4
MEM_HEAD_DIM = 256
MEM_WIDTH = MEM_HEADS * MEM_HEAD_DIM
N_BRANCH = 3
CONV_WIDTH = 4
CHUNK = 128
D_FF = 5632
EPS = 1e-6

OFF_FF = 3 * FOX_WIDTH
OFF_LQ = OFF_FF + FOX_HEADS
OFF_LI = OFF_LQ + 3 * MLSTM_WIDTH
OFF_LO = OFF_LI + 2 * MLSTM_HEADS
N_SMALL = FOX_HEADS + 2 * MLSTM_HEADS
PROJ_BIG = 3 * FOX_WIDTH + 3 * MLSTM_WIDTH + MLSTM_WIDTH + MEM_WIDTH + N_BRANCH * D_MODEL

COL_LQ = 3 * FOX_WIDTH
COL_LK = COL_LQ + MLSTM_WIDTH
COL_LV = COL_LK + MLSTM_WIDTH
COL_LO = COL_LV + MLSTM_WIDTH
COL_MQ = COL_LO + MLSTM_WIDTH
COL_GATES = COL_MQ + MEM_WIDTH

ROW_FOX = 0
ROW_MLI = FOX_HEADS
ROW_MLB = FOX_HEADS + MLSTM_HEADS

V7X_LANES = 128
V7X_VMEM_LIMIT = 56 * 1024 * 1024

F32 = jnp.float32
BF16 = jnp.bfloat16


def _params(*sem):
    return pltpu.CompilerParams(dimension_semantics=sem, vmem_limit_bytes=V7X_VMEM_LIMIT)


def _rms(x, g):
    return x * lax.rsqrt(jnp.mean(x * x, axis=-1, keepdims=True) + EPS) * g


def _dot(a, b):
    return jnp.dot(a, b, preferred_element_type=F32)


def _dot_nt(a, b):
    return lax.dot_general(a, b, (((1,), (1,)), ((), ())), preferred_element_type=F32)


def _dot_tn(a, b):
    return lax.dot_general(a, b, (((0,), (0,)), ((), ())), preferred_element_type=F32)


def _log_sigmoid(x):
    return jnp.minimum(x, 0.0) - jnp.log1p(jnp.exp(-jnp.abs(x)))


SUBLANES = 8


def _norm_in_kernel(x_ref, g_ref, wf_ref, wl_ref, bst_ref, u_ref, gst_ref, wt_ref):
    @pl.when(pl.program_id(0) == 0)
    def _():
        wt_ref[...] = jnp.concatenate([wf_ref[...], wl_ref[...]], axis=0).astype(BF16)

    u = _rms(x_ref[...], g_ref[...]).astype(BF16)
    u_ref[...] = u
    gst_ref[...] = _dot_nt(wt_ref[...], u) + bst_ref[...]


def _norm_in(x2, g, w_in_t, b_small_t, bm=1024):
    m = x2.shape[0]
    assert FOX_HEADS == SUBLANES and 2 * MLSTM_HEADS == SUBLANES
    return pl.pallas_call(
        _norm_in_kernel,
        grid=(m // bm,),
        in_specs=[
            pl.BlockSpec((bm, D_MODEL), lambda i: (i, 0)),
            pl.BlockSpec((1, D_MODEL), lambda i: (0, 0)),
            pl.BlockSpec((SUBLANES, D_MODEL), lambda i: (OFF_FF // SUBLANES, 0)),
            pl.BlockSpec((SUBLANES, D_MODEL), lambda i: (OFF_LI // SUBLANES, 0)),
            pl.BlockSpec((N_SMALL, 1), lambda i: (0, 0)),
        ],
        out_specs=[
            pl.BlockSpec((bm, D_MODEL), lambda i: (i, 0)),
            pl.BlockSpec((N_SMALL, bm), lambda i: (0, i)),
        ],
        out_shape=[
            jax.ShapeDtypeStruct((m, D_MODEL), BF16),
            jax.ShapeDtypeStruct((N_SMALL, m), F32),
        ],
        scratch_shapes=[pltpu.VMEM((N_SMALL, D_MODEL), BF16)],
        compiler_params=_params("arbitrary"),
        name="norm_in",
    )(x2, g, w_in_t, w_in_t, b_small_t)


CAST_ROWS = 256
EPILOGUE_ROWS = 256
PLAIN_ROWS = 512


def _cast_weight(dst_ref, src_fn, rows):
    def chunk(c, carry):
        sl = pl.ds(pl.multiple_of(c * CAST_ROWS, CAST_ROWS), CAST_ROWS)
        dst_ref[sl, :] = src_fn(sl).astype(BF16)
        return carry

    lax.fori_loop(0, rows // CAST_ROWS, chunk, 0)


def _in_proj_kernel(u_ref, w_ref, wx_ref, b_ref, cw_ref, cb_ref, o_ref, wb_ref, xs_ref,
                    *, bn, tiles_per_seq):
    j = pl.program_id(0)
    first_lq = OFF_FF // bn
    first_lo = (OFF_LI - FOX_HEADS) // bn

    def cast_shifted(shift):
        def body():
            def chunk(c, carry):
                src = pl.ds(pl.multiple_of(c * CAST_ROWS + shift, SUBLANES), CAST_ROWS)
                dst = pl.ds(pl.multiple_of(c * CAST_ROWS, CAST_ROWS), CAST_ROWS)
                wb_ref[dst, :] = w_ref[src, :].astype(BF16)
                return carry

            lax.fori_loop(0, bn // CAST_ROWS - 1, chunk, 0)
            last = bn - CAST_ROWS
            tail = jnp.concatenate([w_ref[last + shift:bn, :], wx_ref[0:shift, :]], axis=0)
            wb_ref[last:bn, :] = tail.astype(BF16)
        return body

    @pl.when(pl.program_id(1) == 0)
    def _():
        @pl.when(j < first_lq)
        def _():
            _cast_weight(wb_ref, lambda sl: w_ref[sl, :], bn)

        pl.when(jnp.logical_and(j >= first_lq, j < first_lo))(cast_shifted(FOX_HEADS))
        pl.when(j >= first_lo)(cast_shifted(N_SMALL))

    i = pl.program_id(1)
    bm = u_ref.shape[0]
    t_lq, t_lk, t_lo, t_mq = (c // bn for c in (COL_LQ, COL_LK, COL_LO, COL_MQ))
    is_conv = jnp.logical_or(j == t_lq, j == t_lk)
    is_gate = jnp.logical_or(j == t_lo, j > t_mq)

    row_tiles = [slice(r0, r0 + EPILOGUE_ROWS) for r0 in range(0, bm, EPILOGUE_ROWS)]

    def raw(rs):
        return _dot_nt(u_ref[rs, :], wb_ref[...]) + b_ref[...]

    @pl.when(jnp.logical_not(jnp.logical_or(is_conv, is_gate)))
    def _():
        mult = jnp.where(j == 0, FOX_Q_SCALE, jnp.where(j == t_mq, MEM_HEAD_DIM ** -0.5, 1.0))
        for r0 in range(0, bm, PLAIN_ROWS):
            rs = slice(r0, r0 + PLAIN_ROWS)
            o_ref[rs, :] = (raw(rs) * mult).astype(BF16)

    @pl.when(is_gate)
    def _():
        for rs in row_tiles:
            o_ref[rs, :] = jax.nn.sigmoid(raw(rs)).astype(BF16)

    @pl.when(is_conv)
    def _():
        cw = cw_ref[...]
        mult = jnp.where(j == t_lk, MLSTM_HEAD_DIM ** -0.5, 1.0)
        @pl.when(i % tiles_per_seq == 0)
        def _():
            xs_ref[0, 0:SUBLANES, :] = jnp.zeros((SUBLANES, bn), F32)

        for n, rs in enumerate(row_tiles):
            cur, nxt = n % 2, (n + 1) % 2
            x = raw(rs)
            xs_ref[cur, SUBLANES:SUBLANES + EPILOGUE_ROWS, :] = x
            xs_ref[nxt, 0:SUBLANES, :] = x[EPILOGUE_ROWS - SUBLANES:EPILOGUE_ROWS, :]
            y = cb_ref[...] + cw[CONV_WIDTH - 1:CONV_WIDTH, :] * x
            for t in range(1, CONV_WIDTH):
                y = y + (cw[CONV_WIDTH - 1 - t:CONV_WIDTH - t, :]
                         * xs_ref[cur, SUBLANES - t:SUBLANES - t + EPILOGUE_ROWS, :])
            o_ref[rs, :] = (y * jax.nn.sigmoid(y) * mult).astype(BF16)


def _in_proj(u, w_in_t, b_big, conv_w, conv_b, seq, bm=2048, bn=FOX_WIDTH):
    m = u.shape[0]
    assert bn == FOX_WIDTH and OFF_FF % bn == 0 and (OFF_LI - FOX_HEADS) % bn == 0
    assert seq % bm == 0 and MLSTM_WIDTH == bn and CONV_WIDTH - 1 <= SUBLANES
    extra_per_tile = bn // N_SMALL
    t_lq = COL_LQ // bn
    return pl.pallas_call(
        functools.partial(_in_proj_kernel, bn=bn, tiles_per_seq=seq // bm),
        grid=(PROJ_BIG // bn, m // bm),
        in_specs=[
            pl.BlockSpec((bm, D_MODEL), lambda j, i: (i, 0)),
            pl.BlockSpec((bn, D_MODEL), lambda j, i: (j, 0)),
            pl.BlockSpec((N_SMALL, D_MODEL), lambda j, i: ((j + 1) * extra_per_tile, 0)),
            pl.BlockSpec((1, bn), lambda j, i: (0, j)),
            pl.BlockSpec((CONV_WIDTH, bn), lambda j, i: (0, jnp.clip(j - t_lq, 0, 1))),
            pl.BlockSpec((1, bn), lambda j, i: (0, jnp.clip(j - t_lq, 0, 1))),
        ],
        out_specs=pl.BlockSpec((bm, bn), lambda j, i: (i, j)),
        out_shape=jax.ShapeDtypeStruct((m, PROJ_BIG), BF16),
        scratch_shapes=[pltpu.VMEM((bn, D_MODEL), BF16),
                        pltpu.VMEM((2, SUBLANES + EPILOGUE_ROWS, bn), F32)],
        compiler_params=_params("arbitrary", "arbitrary"),
        name="in_proj",
    )(u, w_in_t, w_in_t, b_big, conv_w, conv_b)


def _gates_kernel(gst_ref, rows_ref, cols_ref, *, seq):
    g = gst_ref[...]
    row = lax.broadcasted_iota(jnp.int32, g.shape, 0)
    lane = lax.broadcasted_iota(jnp.int32, g.shape, 1)
    in_chunk = lane % CHUNK
    is_input_gate = jnp.logical_and(row >= ROW_MLI, row < ROW_MLB)
    local = jnp.where(is_input_gate, g, _log_sigmoid(g))
    d = 1
    while d < CHUNK:
        local = local + jnp.where(in_chunk >= d, pltpu.roll(local, d, axis=1), 0.0)
        d *= 2
    carry = pltpu.roll(jnp.where(in_chunk == CHUNK - 1, local, 0.0), 1, axis=1)
    carry = jnp.where(lane == 0, 0.0, carry)
    d = 1
    while d < CHUNK:
        carry = carry + jnp.where(in_chunk >= d, pltpu.roll(carry, d, axis=1), 0.0)
        d *= 2
    while d < seq:
        carry = carry + jnp.concatenate([jnp.zeros((N_SMALL, d), F32), carry[:, :seq - d]], axis=1)
        d *= 2
    out = jnp.where(row < ROW_MLI, local + carry, jnp.where(is_input_gate, g, local))
    rows_ref[0] = out
    padded = jnp.concatenate([out, jnp.zeros((V7X_LANES - N_SMALL, seq), F32)], axis=0)
    cols_ref[0] = padded.T


def _gates(gst, batch, seq):
    return pl.pallas_call(
        functools.partial(_gates_kernel, seq=seq),
        grid=(batch,),
        in_specs=[pl.BlockSpec((N_SMALL, seq), lambda b: (0, b))],
        out_specs=[
            pl.BlockSpec((1, N_SMALL, seq), lambda b: (b, 0, 0)),
            pl.BlockSpec((1, seq, V7X_LANES), lambda b: (b, 0, 0)),
        ],
        out_shape=[
            jax.ShapeDtypeStruct((batch, N_SMALL, seq), F32),
            jax.ShapeDtypeStruct((batch, seq, V7X_LANES), F32),
        ],
        compiler_params=_params("parallel"),
        name="gates",
    )(gst)


FOX_TQ = 1024
FOX_SUB = 128
LOG2E = 1.4426950408889634
FOX_Q_SCALE = FOX_HEAD_DIM ** -0.5 * LOG2E


def _split3(f):
    hi = f.astype(BF16).astype(F32)
    r = f - hi
    mid = r.astype(BF16).astype(F32)
    return hi, mid, r - mid


def _fox_kernel(q_ref, k_ref, v_ref, fcol_ref, o_ref, kaug_ref, vaug_ref, q2_ref, m_ref, acc_ref,
                s_ref, p_ref, alpha_ref, *, tq, sub, seq):
    h = pl.program_id(1)
    dh = FOX_HEAD_DIM
    tk = tq // 2
    nq = seq // tq

    def f_column(rows):
        lane = lax.broadcasted_iota(jnp.int32, (rows.shape[0], V7X_LANES), 1)
        return jnp.sum(jnp.where(lane == h, rows, 0.0), axis=-1, keepdims=True) * LOG2E

    for c in range(seq // tk):
        sl = slice(c * tk, (c + 1) * tk)
        hi, mid, lo = _split3(f_column(fcol_ref[0, sl, :]))
        lane = lax.broadcasted_iota(jnp.int32, (tk, V7X_LANES), 1)
        kaug = jnp.where(lane < 3, 1.0, jnp.where(lane == 3, -hi, jnp.where(lane == 4, -mid,
                         jnp.where(lane == 5, -lo, 0.0))))
        qaug = jnp.where(lane == 0, hi, jnp.where(lane == 1, mid, jnp.where(lane == 2, lo,
                         jnp.where(lane < 6, 1.0, 0.0))))
        k2 = jnp.concatenate([k_ref[0, sl, :].astype(F32), kaug], axis=1)
        kaug_ref[c] = k2.T.astype(BF16)
        vaug_ref[sl, 0:dh] = v_ref[0, sl, :]
        vaug_ref[sl, dh:2 * dh] = jnp.where(lane == 0, 1.0, 0.0).astype(BF16)
        q2_ref[sl, 0:dh] = q_ref[0, sl, :]
        q2_ref[sl, dh:2 * dh] = qaug.astype(BF16)

    m_ref[...] = jnp.full_like(m_ref, -jnp.inf)
    acc_ref[...] = jnp.zeros_like(acc_ref)

    all_subs = [slice(r0, r0 + sub) for r0 in range(0, tq, sub)]
    low_subs = [rs for rs in all_subs if rs.start >= tk]
    steps = []
    for qi in range(nq):
        steps += [(qi, t, None, all_subs) for t in range(2 * qi)]
        steps += [(qi, 2 * qi, 0, all_subs), (qi, 2 * qi + 1, tk, low_subs)]

    def logits(n):
        qi, t, _, subs = steps[n]
        kt = kaug_ref[t]
        for rs in subs:
            s_ref[n % 2, rs, :] = _dot(q2_ref[qi * tq + rs.start:qi * tq + rs.stop, :], kt)

    def softmax(n):
        qi, _, col0, subs = steps[n]
        slot = n % 2
        for rs in subs:
            s = s_ref[slot, rs, :]
            if col0 is not None and rs.start < col0 + tk:
                r = lax.broadcasted_iota(jnp.int32, (sub, tk), 0) + rs.start
                c = lax.broadcasted_iota(jnp.int32, (sub, tk), 1) + col0
                s = jnp.where(c <= r, s, -jnp.inf)
            m_prev = m_ref[qi, rs, :]
            m_new = jnp.maximum(m_prev, jnp.broadcast_to(jnp.max(s, axis=-1, keepdims=True), m_prev.shape))
            alpha_ref[slot, rs, :] = jnp.exp2(m_prev - m_new)
            for c0 in range(0, tk, V7X_LANES):
                cs = slice(c0, c0 + V7X_LANES)
                p_ref[slot, rs, cs] = jnp.exp2(s[:, cs] - m_new).astype(BF16)
            m_ref[qi, rs, :] = m_new

    def values(n):
        qi, t, _, subs = steps[n]
        slot = n % 2
        va = vaug_ref[t * tk:(t + 1) * tk, :]
        for rs in subs:
            pv = _dot(p_ref[slot, rs, :], va)
            alpha = alpha_ref[slot, rs, :]
            for c0 in range(0, 2 * dh, V7X_LANES):
                cs = slice(c0, c0 + V7X_LANES)
                acc_ref[qi, rs, cs] = alpha * acc_ref[qi, rs, cs] + pv[:, cs]
        if n + 1 == len(steps) or steps[n + 1][0] != qi:
            o_ref[0, qi * tq:(qi + 1) * tq, :] = (acc_ref[qi, :, 0:dh]
                                                  / acc_ref[qi, :, dh:dh + 1]).astype(BF16)

    logits(0)
    for n in range(len(steps)):
        if n > 0:
            values(n - 1)
        softmax(n)
        if n + 1 < len(steps):
            logits(n + 1)
    values(len(steps) - 1)


def _fox(proj3, cols, tq=FOX_TQ, sub=FOX_SUB):
    batch, seq, _ = proj3.shape
    dh = FOX_HEAD_DIM
    tk = tq // 2
    nq = seq // tq

    def head(offset):
        return pl.BlockSpec((1, seq, dh), lambda b, h: (b, 0, offset + h))

    return pl.pallas_call(
        functools.partial(_fox_kernel, tq=tq, sub=sub, seq=seq),
        grid=(batch, FOX_HEADS),
        in_specs=[head(0), head(FOX_HEADS), head(2 * FOX_HEADS),
                  pl.BlockSpec((1, seq, V7X_LANES), lambda b, h: (b, 0, 0))],
        out_specs=head(0),
        out_shape=jax.ShapeDtypeStruct((batch, seq, FOX_WIDTH), BF16),
        scratch_shapes=[pltpu.VMEM((seq // tk, 2 * dh, tk), BF16), pltpu.VMEM((seq, 2 * dh), BF16),
                        pltpu.VMEM((seq, 2 * dh), BF16), pltpu.VMEM((nq, tq, V7X_LANES), F32),
                        pltpu.VMEM((nq, tq, 2 * dh), F32), pltpu.VMEM((2, tq, tk), F32),
                        pltpu.VMEM((2, tq, tk), BF16), pltpu.VMEM((2, tq, V7X_LANES), F32)],
        compiler_params=_params("parallel", "parallel"),
        name="fox_attn",
    )(proj3, proj3, proj3, cols)


MLSTM_GROUP = 2


def _mlstm_kernel(q_ref, k_ref, v_ref, o_ref, gn_ref, rows_ref, cols_ref, w0_ref, w1_ref, w2_ref,
                  w3_ref, y_ref, w0b_ref, w1b_ref, w2b_ref, w3b_ref, c_ref, n_ref, m_ref, *, batch):
    j = pl.program_id(0)
    L = CHUNK
    dh = MLSTM_HEAD_DIM

    @pl.when(j == 0)
    def _():
        c_ref[...] = jnp.zeros_like(c_ref)
        n_ref[...] = jnp.zeros_like(n_ref)
        m_ref[...] = jnp.zeros_like(m_ref)

    for src, dst in ((w0_ref, w0b_ref), (w1_ref, w1b_ref), (w2_ref, w2b_ref), (w3_ref, w3b_ref)):
        dst[...] = src[...].astype(BF16)

    lanes = V7X_LANES
    assert L == lanes and dh % lanes == 0
    halves = [slice(c0, c0 + lanes) for c0 in range(0, dh, lanes)]
    r = lax.broadcasted_iota(jnp.int32, (L, L), 0)
    c = lax.broadcasted_iota(jnp.int32, (L, L), 1)
    causal = c <= r
    all_chains = [(b, h) for b in range(batch) for h in range(MLSTM_HEADS)]

    def rep(x):
        return jnp.broadcast_to(x, (L, lanes))

    sel_r = lax.broadcasted_iota(jnp.int32, (3 * lanes, 2 * lanes), 0) % lanes
    sel_c = lax.broadcasted_iota(jnp.int32, (3 * lanes, 2 * lanes), 1)
    pieces = []
    for b in range(batch):
        hi, mid, lo = _split3(cols_ref[b])
        pieces.append(jnp.concatenate([hi, mid, lo], axis=1).astype(BF16))

    def run_group(chains):
        st1 = []
        for b, h in chains:
            hs = slice(h * dh, (h + 1) * dh)
            st = b * MLSTM_HEADS + h
            sel = sel_r == jnp.where(sel_c < lanes, ROW_MLB + h, ROW_MLI + h)
            bi = _dot(pieces[b], jnp.where(sel, 1.0, 0.0).astype(BF16))
            bcol, icol = bi[:, 0:lanes], bi[:, lanes:2 * lanes]
            qb = q_ref[b, :, hs]
            kb = k_ref[b, :, hs]
            brow = rows_ref[b, ROW_MLB + h:ROW_MLB + h + 1, :]
            irow = rows_ref[b, ROW_MLI + h:ROW_MLI + h + 1, :]
            m_prev = m_ref[st]
            c_prev = c_ref[st]
            qk_raw = _dot_nt(qb, kb)
            qc = _dot(qb, c_prev.astype(BF16))

            dlog = jnp.where(causal, bcol - brow + irow, -jnp.inf)
            inter = bcol + m_prev
            m_t = jnp.maximum(inter, rep(jnp.max(dlog, axis=-1, keepdims=True)))
            w = jnp.exp(dlog - m_t)
            a = jnp.exp(inter - m_t)
            b_end = brow[:, L - 1:L]
            m_new = jnp.maximum(b_end + m_prev, jnp.max(b_end - brow + irow, axis=-1, keepdims=True))
            decay = jnp.exp(b_end + m_prev - m_new)
            ws = jnp.exp(b_end - bcol + icol - m_new)
            qn = rep(jnp.sum(qb.astype(F32) * n_ref[st], axis=-1, keepdims=True))
            st1.append((hs, st, kb, c_prev, qk_raw, qc, m_t, w, a, m_new, decay, ws, qn))

        st2 = []
        for (b, h), (hs, st, kb, c_prev, qk_raw, qc, m_t, w, a, m_new, decay, ws, qn) in zip(chains, st1):
            v = v_ref[b, :, hs]
            qk = qk_raw * w
            k = kb.astype(F32)
            kw = jnp.concatenate([k[:, cs] * ws for cs in halves], axis=1)
            pv = _dot(qk.astype(BF16), v)
            upd = _dot_tn(kw.astype(BF16), v)
            st2.append((qk, kw, pv, upd))

        for (b, h), s1, (qk, kw, pv, upd) in zip(chains, st1, st2):
            hs, st, kb, c_prev, qk_raw, qc, m_t, w, a, m_new, decay, ws, qn = s1
            den = a * qn + rep(jnp.sum(qk, axis=-1, keepdims=True))
            inv = 1.0 / jnp.maximum(jnp.abs(den), jnp.exp(-m_t))
            h_parts = [(a * qc[:, cs] + pv[:, cs]) * inv for cs in halves]
            sq = h_parts[0] * h_parts[0]
            for hp in h_parts[1:]:
                sq = sq + hp * hp
            rms = lax.rsqrt(rep(jnp.sum(sq, axis=-1, keepdims=True)) * (1.0 / dh) + EPS)
            for cs, hp in zip(halves, h_parts):
                oc = slice(hs.start + cs.start, hs.start + cs.stop)
                y = hp * rms * gn_ref[:, oc] * o_ref[b, :, oc].astype(F32)
                y_ref[b, :, oc] = y.astype(BF16)
            c_ref[st] = decay * c_prev + upd
            n_ref[st] = decay * n_ref[st] + jnp.sum(kw, axis=0, keepdims=True)
            m_ref[st] = m_new

    for g0 in range(0, len(all_chains), MLSTM_GROUP):
        run_group(all_chains[g0:g0 + MLSTM_GROUP])


def _mlstm(proj3, gn, rows, cols, weights):
    batch, seq, _ = proj3.shape
    dh = MLSTM_HEAD_DIM
    width = MLSTM_WIDTH
    nc = seq // CHUNK
    chains = batch * MLSTM_HEADS

    def col(base):
        return lambda j: (0, j, base // width)

    shares = [w.shape[0] // nc for w in weights]
    assert all(s * nc == w.shape[0] and s % 16 == 0 for s, w in zip(shares, weights))
    w_specs = [pl.BlockSpec((s, w.shape[1]), lambda j: (j, 0)) for s, w in zip(shares, weights)]

    return pl.pallas_call(
        functools.partial(_mlstm_kernel, batch=batch),
        grid=(nc,),
        in_specs=[
            pl.BlockSpec((batch, CHUNK, width), col(COL_LQ)),
            pl.BlockSpec((batch, CHUNK, width), col(COL_LK)),
            pl.BlockSpec((batch, CHUNK, width), col(COL_LV)),
            pl.BlockSpec((batch, CHUNK, width), col(COL_LO)),
            pl.BlockSpec((1, width), lambda j: (0, 0)),
            pl.BlockSpec((batch, N_SMALL, CHUNK), lambda j: (0, 0, j)),
            pl.BlockSpec((batch, CHUNK, V7X_LANES), lambda j: (0, j, 0)),
        ] + w_specs,
        out_specs=[pl.BlockSpec((batch, CHUNK, width), lambda j: (0, j, 0))] + w_specs,
        out_shape=[jax.ShapeDtypeStruct((batch, seq, width), BF16)]
        + [jax.ShapeDtypeStruct(w.shape, BF16) for w in weights],
        scratch_shapes=[pltpu.VMEM((chains, dh, dh), F32), pltpu.VMEM((chains, 1, dh), F32),
                        pltpu.VMEM((chains, 1, 1), F32)],
        compiler_params=_params("arbitrary"),
        name="mlstm",
    )(proj3, proj3, proj3, proj3, gn, rows, cols, *weights)


def _mem_kv_kernel(mem_ref, g_ref, w_ref, o_ref):
    u = _rms(mem_ref[...], g_ref[...]).astype(BF16)
    o_ref[...] = _dot(u, w_ref[...].astype(BF16)).astype(BF16)


def _mem_kv(mem2, g, w, bn=512):
    m = mem2.shape[0]
    n = w.shape[1]
    return pl.pallas_call(
        _mem_kv_kernel,
        grid=(n // bn,),
        in_specs=[
            pl.BlockSpec((m, D_MODEL), lambda j: (0, 0)),
            pl.BlockSpec((1, D_MODEL), lambda j: (0, 0)),
            pl.BlockSpec((D_MODEL, bn), lambda j: (0, j)),
        ],
        out_specs=pl.BlockSpec((m, bn), lambda j: (0, j)),
        out_shape=jax.ShapeDtypeStruct((m, n), BF16),
        compiler_params=_params("parallel"),
        name="mem_kv",
    )(mem2, g, w)


def _merge_out_kernel(yf_ref, ym_ref, mq_ref, mk_ref, mv_ref, g0_ref, g1_ref, g2_ref, wf_ref, wm_ref,
                      wc_ref, wo_ref, x_ref, gn_ref, h_ref, u_ref):
    dh = MEM_HEAD_DIM
    heads = [slice(hh * dh, (hh + 1) * dh) for hh in range(MEM_HEADS)]
    logits = [_dot_nt(mq_ref[:, sl], mk_ref[0, :, sl]) for sl in heads]
    merged = g0_ref[...].astype(F32) * _dot(yf_ref[...], wf_ref[...])
    y_mem = []
    for sl, s in zip(heads, logits):
        p = jnp.exp(s - jnp.max(s, axis=-1, keepdims=True))
        l = jnp.sum(p, axis=-1, keepdims=True)
        y_mem.append((_dot(p.astype(BF16), mv_ref[0, :, sl]) / l).astype(BF16))
    merged = merged + g1_ref[...].astype(F32) * _dot(ym_ref[...], wm_ref[...])
    merged = merged + g2_ref[...].astype(F32) * _dot(jnp.concatenate(y_mem, axis=1), wc_ref[...])
    h = x_ref[...] + _dot(merged.astype(BF16), wo_ref[...])
    h_ref[...] = h
    u_ref[...] = _rms(h, gn_ref[...]).astype(BF16)


def _merge_out(y_fox, y_ml, proj, mkv3, w_f, w_m, w_c, w_o, x2, g_ffn, seq, bm=256):
    m = x2.shape[0]
    kdim = y_fox.shape[1]
    mem_len = mkv3.shape[1]
    assert seq % bm == 0
    tiles_per_seq = seq // bm
    y_spec = pl.BlockSpec((bm, kdim), lambda i: (i, 0))
    row_spec = pl.BlockSpec((bm, D_MODEL), lambda i: (i, 0))

    def gate_spec(branch):
        return pl.BlockSpec((bm, D_MODEL), lambda i: (i, COL_GATES // D_MODEL + branch))

    def resident(rows):
        return pl.BlockSpec((rows, D_MODEL), lambda i: (0, 0), pipeline_mode=pl.Buffered(1))

    def mem_spec(half):
        return pl.BlockSpec((1, mem_len, MEM_WIDTH), lambda i: (i // tiles_per_seq, 0, half))

    return pl.pallas_call(
        _merge_out_kernel,
        grid=(m // bm,),
        in_specs=[y_spec, y_spec,
                  pl.BlockSpec((bm, MEM_WIDTH), lambda i: (i, COL_MQ // MEM_WIDTH)),
                  mem_spec(0), mem_spec(1),
                  gate_spec(0), gate_spec(1), gate_spec(2),
                  resident(kdim), resident(kdim), resident(kdim), resident(D_MODEL),
                  row_spec, pl.BlockSpec((1, D_MODEL), lambda i: (0, 0))],
        out_specs=[row_spec, row_spec],
        out_shape=[
            jax.ShapeDtypeStruct((m, D_MODEL), F32),
            jax.ShapeDtypeStruct((m, D_MODEL), BF16),
        ],
        compiler_params=_params("parallel"),
        name="merge_out",
    )(y_fox, y_ml, proj, mkv3, mkv3, proj, proj, proj, w_f, w_m, w_c, w_o, x2, g_ffn)


def _ffn_in_kernel(u_ref, wg_ref, wu_ref, wo_ref, o_ref, wob_ref, wgb_ref, wub_ref):
    @pl.when(pl.program_id(1) == 0)
    def _():
        _cast_weight(wgb_ref, lambda sl: wg_ref[sl, :], D_MODEL)
        _cast_weight(wub_ref, lambda sl: wu_ref[sl, :], D_MODEL)

    wob_ref[...] = wo_ref[...].astype(BF16)
    for r0 in range(0, u_ref.shape[0], EPILOGUE_ROWS):
        rs = slice(r0, r0 + EPILOGUE_ROWS)
        u = u_ref[rs, :]
        gate = _dot(u, wgb_ref[...])
        up = _dot(u, wub_ref[...])
        o_ref[rs, :] = (gate * jax.nn.sigmoid(gate) * up).astype(BF16)


def _ffn_in(u, w, w_out_f32, bm=1024, bn=512):
    m = u.shape[0]
    nb = D_FF // bn
    mt = m // bm
    share = D_FF // (nb * mt)
    assert share * nb * mt == D_FF and share % 16 == 0
    return pl.pallas_call(
        _ffn_in_kernel,
        grid=(nb, mt),
        in_specs=[
            pl.BlockSpec((bm, D_MODEL), lambda j, i: (i, 0)),
            pl.BlockSpec((D_MODEL, bn), lambda j, i: (0, j)),
            pl.BlockSpec((D_MODEL, bn), lambda j, i: (0, nb + j)),
            pl.BlockSpec((share, D_MODEL), lambda j, i: (j * mt + i, 0)),
        ],
        out_specs=[
            pl.BlockSpec((bm, bn), lambda j, i: (i, j)),
            pl.BlockSpec((share, D_MODEL), lambda j, i: (j * mt + i, 0)),
        ],
        out_shape=[
            jax.ShapeDtypeStruct((m, D_FF), BF16),
            jax.ShapeDtypeStruct((D_FF, D_MODEL), BF16),
        ],
        scratch_shapes=[pltpu.VMEM((D_MODEL, bn), BF16)] * 2,
        compiler_params=_params("arbitrary", "arbitrary"),
        name="ffn_in",
    )(u, w, w, w_out_f32)


def _ffn_out_kernel(a_ref, w_ref, r_ref, g_ref, o_ref, *, final_norm):
    h = r_ref[...] + _dot(a_ref[...], w_ref[...])
    o_ref[...] = _rms(h, g_ref[...]) if final_norm else h


def _ffn_out(act, w_bf16, resid, final_gain, bm=256):
    m = act.shape[0]
    final_norm = final_gain is not None
    gain = final_gain if final_norm else jnp.ones((1, D_MODEL), F32)
    return pl.pallas_call(
        functools.partial(_ffn_out_kernel, final_norm=final_norm),
        grid=(m // bm,),
        in_specs=[
            pl.BlockSpec((bm, D_FF), lambda i: (i, 0)),
            pl.BlockSpec((D_FF, D_MODEL), lambda i: (0, 0), pipeline_mode=pl.Buffered(1)),
            pl.BlockSpec((bm, D_MODEL), lambda i: (i, 0)),
            pl.BlockSpec((1, D_MODEL), lambda i: (0, 0)),
        ],
        out_specs=pl.BlockSpec((bm, D_MODEL), lambda i: (i, 0)),
        out_shape=jax.ShapeDtypeStruct((m, D_MODEL), F32),
        compiler_params=_params("parallel"),
        name="ffn_out",
    )(act, w_bf16, resid, gain)


def _layer(h2, mem2, batch, seq, norm_mix, w_in, b_in, conv_w, conv_b, mlstm_norm, norm_mem,
           w_mem_kv, w_br_fox, w_br_mlstm, w_br_mem, w_out, norm_ffn, w_ffn_in, w_ffn_out,
           final_gain):
    b_big = jnp.concatenate([b_in[:OFF_FF], b_in[OFF_LQ:OFF_LI], b_in[OFF_LO:]])[None, :]
    b_small_t = jnp.concatenate([b_in[OFF_FF:OFF_LQ], b_in[OFF_LI:OFF_LO]])[:, None]

    w_in_t = w_in.T
    u, gst = _norm_in(h2, norm_mix[None, :], w_in_t, b_small_t)
    proj = _in_proj(u, w_in_t, b_big, conv_w, conv_b[None, :], seq)
    rows, cols = _gates(gst, batch, seq)
    proj3 = proj.reshape(batch, seq, PROJ_BIG)

    y_fox = _fox(proj3, cols)
    y_ml, wf_b, wm_b, wc_b, wo_b = _mlstm(proj3, mlstm_norm[None, :], rows, cols,
                                          (w_br_fox, w_br_mlstm, w_br_mem, w_out))
    mkv = _mem_kv(mem2, norm_mem[None, :], w_mem_kv)

    tokens = batch * seq
    h2, u_ffn = _merge_out(y_fox.reshape(tokens, FOX_WIDTH), y_ml.reshape(tokens, MLSTM_WIDTH),
                           proj, mkv.reshape(batch, -1, 2 * MEM_WIDTH), wf_b, wm_b, wc_b, wo_b,
                           h2, norm_ffn[None, :], seq)
    act, w_ffn_out_bf16 = _ffn_in(u_ffn, w_ffn_in, w_ffn_out)
    return _ffn_out(act, w_ffn_out_bf16, h2, final_gain)


def kernel(x, mem, norm_mix, w_in, b_in, conv_w, conv_b, mlstm_norm, norm_mem, w_mem_kv, w_br_fox,
           w_br_mlstm, w_br_mem, w_out, norm_ffn, w_ffn_in, w_ffn_out, norm_final):
    batch, seq, d = x.shape
    assert d == D_MODEL and seq % CHUNK == 0
    h2 = x.reshape(batch * seq, d)
    mem2 = mem.reshape(batch * mem.shape[1], d)
    depth = norm_mix.shape[0]
    for l in range(depth):
        final_gain = norm_final[None, :] if l == depth - 1 else None
        h2 = _layer(h2, mem2, batch, seq, norm_mix[l], w_in[l], b_in[l], conv_w[l], conv_b[l],
                    mlstm_norm[l], norm_mem[l], w_mem_kv[l], w_br_fox[l], w_br_mlstm[l],
                    w_br_mem[l], w_out[l], norm_ffn[l], w_ffn_in[l], w_ffn_out[l], final_gain)
    return h2.reshape(batch, seq, d)
```

```python
import functools

import jax
import jax.numpy as jnp
from jax import lax
from jax.experimental import pallas as pl
from jax.experimental.pallas import tpu as pltpu

D_MODEL = 2048
FOX_HEADS = 8
FOX_HEAD_DIM = 128
FOX_WIDTH = FOX_HEADS * FOX_HEAD_DIM
MLSTM_HEADS = 4
MLSTM_HEAD_DIM = 256
MLSTM_WIDTH = MLSTM_HEADS * MLSTM_HEAD_DIM
MEM_HEADS = 4
MEM_HEAD_DIM = 256
MEM_WIDTH = MEM_HEADS * MEM_HEAD_DIM
N_BRANCH = 3
CONV_WIDTH = 4
CHUNK = 128
D_FF = 5632
EPS = 1e-6

OFF_FF = 3 * FOX_WIDTH
OFF_LQ = OFF_FF + FOX_HEADS
OFF_LI = OFF_LQ + 3 * MLSTM_WIDTH
OFF_LO = OFF_LI + 2 * MLSTM_HEADS
N_SMALL = FOX_HEADS + 2 * MLSTM_HEADS
PROJ_BIG = 3 * FOX_WIDTH + 3 * MLSTM_WIDTH + MLSTM_WIDTH + MEM_WIDTH + N_BRANCH * D_MODEL

COL_LQ = 3 * FOX_WIDTH
COL_LK = COL_LQ + MLSTM_WIDTH
COL_LV = COL_LK + MLSTM_WIDTH
COL_LO = COL_LV + MLSTM_WIDTH
COL_MQ = COL_LO + MLSTM_WIDTH
COL_GATES = COL_MQ + MEM_WIDTH

ROW_FOX = 0
ROW_MLI = FOX_HEADS
ROW_MLB = FOX_HEADS + MLSTM_HEADS

V7X_LANES = 128
V7X_VMEM_LIMIT = 56 * 1024 * 1024

F32 = jnp.float32
BF16 = jnp.bfloat16


def _params(*sem):
    return pltpu.CompilerParams(dimension_semantics=sem, vmem_limit_bytes=V7X_VMEM_LIMIT)


def _rms(x, g):
    return x * lax.rsqrt(jnp.mean(x * x, axis=-1, keepdims=True) + EPS) * g


def _dot(a, b):
    return jnp.dot(a, b, preferred_element_type=F32)


def _dot_nt(a, b):
    return lax.dot_general(a, b, (((1,), (1,)), ((), ())), preferred_element_type=F32)


def _dot_tn(a, b):
    return lax.dot_general(a, b, (((0,), (0,)), ((), ())), preferred_element_type=F32)


def _log_sigmoid(x):
    return jnp.minimum(x, 0.0) - jnp.log1p(jnp.exp(-jnp.abs(x)))


SUBLANES = 8


def _norm_in_kernel(x_ref, g_ref, wf_ref, wl_ref, bst_ref, mem_ref, gm_ref, wkv_ref,
                    u_ref, gst_ref, mkv_ref, wt_ref, um_ref, *, kv_tiles):
    i = pl.program_id(0)

    @pl.when(i == 0)
    def _():
        wt_ref[...] = jnp.concatenate([wf_ref[...], wl_ref[...]], axis=0).astype(BF16)
        um_ref[...] = _rms(mem_ref[...], gm_ref[...]).astype(BF16)

    @pl.when(i < kv_tiles)
    def _():
        mkv_ref[...] = _dot(um_ref[...], wkv_ref[...].astype(BF16)).astype(BF16)

    u = _rms(x_ref[...], g_ref[...]).astype(BF16)
    u_ref[...] = u
    gst_ref[...] = _dot_nt(wt_ref[...], u) + bst_ref[...]


def _norm_in(x2, g, w_in_t, b_small_t, mem2, g_mem, w_mem_kv, bm=1024, bn=512):
    m = x2.shape[0]
    mm = mem2.shape[0]
    n_kv = w_mem_kv.shape[1]
    kv_tiles = n_kv // bn
    assert FOX_HEADS == SUBLANES and 2 * MLSTM_HEADS == SUBLANES and kv_tiles <= m // bm
    last = kv_tiles - 1
    return pl.pallas_call(
        functools.partial(_norm_in_kernel, kv_tiles=kv_tiles),
        grid=(m // bm,),
        in_specs=[
            pl.BlockSpec((bm, D_MODEL), lambda i: (i, 0)),
            pl.BlockSpec((1, D_MODEL), lambda i: (0, 0)),
            pl.BlockSpec((SUBLANES, D_MODEL), lambda i: (OFF_FF // SUBLANES, 0)),
            pl.BlockSpec((SUBLANES, D_MODEL), lambda i: (OFF_LI // SUBLANES, 0)),
            pl.BlockSpec((N_SMALL, 1), lambda i: (0, 0)),
            pl.BlockSpec((mm, D_MODEL), lambda i: (0, 0)),
            pl.BlockSpec((1, D_MODEL), lambda i: (0, 0)),
            pl.BlockSpec((D_MODEL, bn), lambda i: (0, jnp.minimum(i, last))),
        ],
        out_specs=[
            pl.BlockSpec((bm, D_MODEL), lambda i: (i, 0)),
            pl.BlockSpec((N_SMALL, bm), lambda i: (0, i)),
            pl.BlockSpec((mm, bn), lambda i: (0, jnp.minimum(i, last))),
        ],
        out_shape=[
            jax.ShapeDtypeStruct((m, D_MODEL), BF16),
            jax.ShapeDtypeStruct((N_SMALL, m), F32),
            jax.ShapeDtypeStruct((mm, n_kv), BF16),
        ],
        scratch_shapes=[pltpu.VMEM((N_SMALL, D_MODEL), BF16), pltpu.VMEM((mm, D_MODEL), BF16)],
        compiler_params=_params("arbitrary"),
        name="norm_in",
    )(x2, g, w_in_t, w_in_t, b_small_t, mem2, g_mem, w_mem_kv)


CAST_ROWS = 256
EPILOGUE_ROWS = 256
PLAIN_ROWS = 512


def _cast_weight(dst_ref, src_fn, rows):
    def chunk(c, carry):
        sl = pl.ds(pl.multiple_of(c * CAST_ROWS, CAST_ROWS), CAST_ROWS)
        dst_ref[sl, :] = src_fn(sl).astype(BF16)
        return carry

    lax.fori_loop(0, rows // CAST_ROWS, chunk, 0)


def _in_proj_kernel(u_ref, w_ref, wx_ref, b_ref, cw_ref, cb_ref, o_ref, wb_ref, xs_ref,
                    *, bn, tiles_per_seq):
    j = pl.program_id(0)
    first_lq = OFF_FF // bn
    first_lo = (OFF_LI - FOX_HEADS) // bn

    def cast_shifted(shift):
        def body():
            def chunk(c, carry):
                src = pl.ds(pl.multiple_of(c * CAST_ROWS + shift, SUBLANES), CAST_ROWS)
                dst = pl.ds(pl.multiple_of(c * CAST_ROWS, CAST_ROWS), CAST_ROWS)
                wb_ref[dst, :] = w_ref[src, :].astype(BF16)
                return carry

            lax.fori_loop(0, bn // CAST_ROWS - 1, chunk, 0)
            last = bn - CAST_ROWS
            tail = jnp.concatenate([w_ref[last + shift:bn, :], wx_ref[0:shift, :]], axis=0)
            wb_ref[last:bn, :] = tail.astype(BF16)
        return body

    @pl.when(pl.program_id(1) == 0)
    def _():
        @pl.when(j < first_lq)
        def _():
            _cast_weight(wb_ref, lambda sl: w_ref[sl, :], bn)

        pl.when(jnp.logical_and(j >= first_lq, j < first_lo))(cast_shifted(FOX_HEADS))
        pl.when(j >= first_lo)(cast_shifted(N_SMALL))

    i = pl.program_id(1)
    bm = u_ref.shape[0]
    t_lq, t_lk, t_lo, t_mq = (c // bn for c in (COL_LQ, COL_LK, COL_LO, COL_MQ))
    is_conv = jnp.logical_or(j == t_lq, j == t_lk)
    is_gate = jnp.logical_or(j == t_lo, j > t_mq)

    row_tiles = [slice(r0, r0 + EPILOGUE_ROWS) for r0 in range(0, bm, EPILOGUE_ROWS)]

    def raw(rs):
        return _dot_nt(u_ref[rs, :], wb_ref[...]) + b_ref[...]

    @pl.when(jnp.logical_not(jnp.logical_or(is_conv, is_gate)))
    def _():
        mult = jnp.where(j == 0, FOX_Q_SCALE, jnp.where(j == t_mq, MEM_HEAD_DIM ** -0.5, 1.0))
        for r0 in range(0, bm, PLAIN_ROWS):
            rs = slice(r0, r0 + PLAIN_ROWS)
            o_ref[rs, :] = (raw(rs) * mult).astype(BF16)

    @pl.when(is_gate)
    def _():
        for rs in row_tiles:
            o_ref[rs, :] = jax.nn.sigmoid(raw(rs)).astype(BF16)

    @pl.when(is_conv)
    def _():
        cw = cw_ref[...]
        mult = jnp.where(j == t_lk, MLSTM_HEAD_DIM ** -0.5, 1.0)
        @pl.when(i % tiles_per_seq == 0)
        def _():
            xs_ref[0, 0:SUBLANES, :] = jnp.zeros((SUBLANES, bn), F32)

        for n, rs in enumerate(row_tiles):
            cur, nxt = n % 2, (n + 1) % 2
            x = raw(rs)
            xs_ref[cur, SUBLANES:SUBLANES + EPILOGUE_ROWS, :] = x
            xs_ref[nxt, 0:SUBLANES, :] = x[EPILOGUE_ROWS - SUBLANES:EPILOGUE_ROWS, :]
            y = cb_ref[...] + cw[CONV_WIDTH - 1:CONV_WIDTH, :] * x
            for t in range(1, CONV_WIDTH):
                y = y + (cw[CONV_WIDTH - 1 - t:CONV_WIDTH - t, :]
                         * xs_ref[cur, SUBLANES - t:SUBLANES - t + EPILOGUE_ROWS, :])
            o_ref[rs, :] = (y * jax.nn.sigmoid(y) * mult).astype(BF16)


def _in_proj(u, w_in_t, b_big, conv_w, conv_b, seq, bm=2048, bn=FOX_WIDTH):
    m = u.shape[0]
    assert bn == FOX_WIDTH and OFF_FF % bn == 0 and (OFF_LI - FOX_HEADS) % bn == 0
    assert seq % bm == 0 and MLSTM_WIDTH == bn and CONV_WIDTH - 1 <= SUBLANES
    extra_per_tile = bn // N_SMALL
    t_lq = COL_LQ // bn
    return pl.pallas_call(
        functools.partial(_in_proj_kernel, bn=bn, tiles_per_seq=seq // bm),
        grid=(PROJ_BIG // bn, m // bm),
        in_specs=[
            pl.BlockSpec((bm, D_MODEL), lambda j, i: (i, 0)),
            pl.BlockSpec((bn, D_MODEL), lambda j, i: (j, 0)),
            pl.BlockSpec((N_SMALL, D_MODEL), lambda j, i: ((j + 1) * extra_per_tile, 0)),
            pl.BlockSpec((1, bn), lambda j, i: (0, j)),
            pl.BlockSpec((CONV_WIDTH, bn), lambda j, i: (0, jnp.clip(j - t_lq, 0, 1))),
            pl.BlockSpec((1, bn), lambda j, i: (0, jnp.clip(j - t_lq, 0, 1))),
        ],
        out_specs=pl.BlockSpec((bm, bn), lambda j, i: (i, j)),
        out_shape=jax.ShapeDtypeStruct((m, PROJ_BIG), BF16),
        scratch_shapes=[pltpu.VMEM((bn, D_MODEL), BF16),
                        pltpu.VMEM((2, SUBLANES + EPILOGUE_ROWS, bn), F32)],
        compiler_params=_params("arbitrary", "arbitrary"),
        name="in_proj",
    )(u, w_in_t, w_in_t, b_big, conv_w, conv_b)


def _gates_kernel(gst_ref, rows_ref, cols_ref, *, seq):
    g = gst_ref[...]
    row = lax.broadcasted_iota(jnp.int32, g.shape, 0)
    lane = lax.broadcasted_iota(jnp.int32, g.shape, 1)
    in_chunk = lane % CHUNK
    is_input_gate = jnp.logical_and(row >= ROW_MLI, row < ROW_MLB)
    local = jnp.where(is_input_gate, g, _log_sigmoid(g))
    d = 1
    while d < CHUNK:
        local = local + jnp.where(in_chunk >= d, pltpu.roll(local, d, axis=1), 0.0)
        d *= 2
    carry = pltpu.roll(jnp.where(in_chunk == CHUNK - 1, local, 0.0), 1, axis=1)
    carry = jnp.where(lane == 0, 0.0, carry)
    d = 1
    while d < CHUNK:
        carry = carry + jnp.where(in_chunk >= d, pltpu.roll(carry, d, axis=1), 0.0)
        d *= 2
    while d < seq:
        carry = carry + jnp.concatenate([jnp.zeros((N_SMALL, d), F32), carry[:, :seq - d]], axis=1)
        d *= 2
    out = jnp.where(row < ROW_MLI, local + carry, jnp.where(is_input_gate, g, local))
    rows_ref[0] = out
    padded = jnp.concatenate([out, jnp.zeros((V7X_LANES - N_SMALL, seq), F32)], axis=0)
    cols_ref[0] = padded.T


def _gates(gst, batch, seq):
    return pl.pallas_call(
        functools.partial(_gates_kernel, seq=seq),
        grid=(batch,),
        in_specs=[pl.BlockSpec((N_SMALL, seq), lambda b: (0, b))],
        out_specs=[
            pl.BlockSpec((1, N_SMALL, seq), lambda b: (b, 0, 0)),
            pl.BlockSpec((1, seq, V7X_LANES), lambda b: (b, 0, 0)),
        ],
        out_shape=[
            jax.ShapeDtypeStruct((batch, N_SMALL, seq), F32),
            jax.ShapeDtypeStruct((batch, seq, V7X_LANES), F32),
        ],
        compiler_params=_params("parallel"),
        name="gates",
    )(gst)


FOX_TQ = 1024
FOX_SUB = 128
LOG2E = 1.4426950408889634
FOX_Q_SCALE = FOX_HEAD_DIM ** -0.5 * LOG2E


def _split3(f):
    hi = f.astype(BF16).astype(F32)
    r = f - hi
    mid = r.astype(BF16).astype(F32)
    return hi, mid, r - mid


def _fox_kernel(q_ref, k_ref, v_ref, fcol_ref, o_ref, kaug_ref, vaug_ref, q2_ref, m_ref, acc_ref,
                s_ref, p_ref, alpha_ref, *, tq, sub, seq):
    h = pl.program_id(1)
    dh = FOX_HEAD_DIM
    tk = tq // 2
    nq = seq // tq

    def f_column(rows):
        lane = lax.broadcasted_iota(jnp.int32, (rows.shape[0], V7X_LANES), 1)
        return jnp.sum(jnp.where(lane == h, rows, 0.0), axis=-1, keepdims=True) * LOG2E

    for c in range(seq // tk):
        sl = slice(c * tk, (c + 1) * tk)
        hi, mid, lo = _split3(f_column(fcol_ref[0, sl, :]))
        lane = lax.broadcasted_iota(jnp.int32, (tk, V7X_LANES), 1)
        kaug = jnp.where(lane < 3, 1.0, jnp.where(lane == 3, -hi, jnp.where(lane == 4, -mid,
                         jnp.where(lane == 5, -lo, 0.0))))
        qaug = jnp.where(lane == 0, hi, jnp.where(lane == 1, mid, jnp.where(lane == 2, lo,
                         jnp.where(lane < 6, 1.0, 0.0))))
        k2 = jnp.concatenate([k_ref[0, sl, :].astype(F32), kaug], axis=1)
        kaug_ref[c] = k2.T.astype(BF16)
        vaug_ref[sl, 0:dh] = v_ref[0, sl, :]
        vaug_ref[sl, dh:2 * dh] = jnp.where(lane == 0, 1.0, 0.0).astype(BF16)
        q2_ref[sl, 0:dh] = q_ref[0, sl, :]
        q2_ref[sl, dh:2 * dh] = qaug.astype(BF16)

    m_ref[...] = jnp.full_like(m_ref, -jnp.inf)
    acc_ref[...] = jnp.zeros_like(acc_ref)

    all_subs = [slice(r0, r0 + sub) for r0 in range(0, tq, sub)]
    low_subs = [rs for rs in all_subs if rs.start >= tk]
    steps = []
    for qi in range(nq):
        steps += [(qi, t, None, all_subs) for t in range(2 * qi)]
        steps += [(qi, 2 * qi, 0, all_subs), (qi, 2 * qi + 1, tk, low_subs)]

    def logits(n):
        qi, t, _, subs = steps[n]
        kt = kaug_ref[t]
        for rs in subs:
            s_ref[n % 2, rs, :] = _dot(q2_ref[qi * tq + rs.start:qi * tq + rs.stop, :], kt)

    def softmax(n):
        qi, _, col0, subs = steps[n]
        slot = n % 2
        for rs in subs:
            s = s_ref[slot, rs, :]
            if col0 is not None and rs.start < col0 + tk:
                r = lax.broadcasted_iota(jnp.int32, (sub, tk), 0) + rs.start
                c = lax.broadcasted_iota(jnp.int32, (sub, tk), 1) + col0
                s = jnp.where(c <= r, s, -jnp.inf)
            m_prev = m_ref[qi, rs, :]
            m_new = jnp.maximum(m_prev, jnp.broadcast_to(jnp.max(s, axis=-1, keepdims=True), m_prev.shape))
            alpha_ref[slot, rs, :] = jnp.exp2(m_prev - m_new)
            for c0 in range(0, tk, V7X_LANES):
                cs = slice(c0, c0 + V7X_LANES)
                p_ref[slot, rs, cs] = jnp.exp2(s[:, cs] - m_new).astype(BF16)
            m_ref[qi, rs, :] = m_new

    def values(n):
        qi, t, _, subs = steps[n]
        slot = n % 2
        va = vaug_ref[t * tk:(t + 1) * tk, :]
        for rs in subs:
            pv = _dot(p_ref[slot, rs, :], va)
            alpha = alpha_ref[slot, rs, :]
            for c0 in range(0, 2 * dh, V7X_LANES):
                cs = slice(c0, c0 + V7X_LANES)
                acc_ref[qi, rs, cs] = alpha * acc_ref[qi, rs, cs] + pv[:, cs]
        if n + 1 == len(steps) or steps[n + 1][0] != qi:
            o_ref[0, qi * tq:(qi + 1) * tq, :] = (acc_ref[qi, :, 0:dh]
                                                  / acc_ref[qi, :, dh:dh + 1]).astype(BF16)

    logits(0)
    for n in range(len(steps)):
        if n > 0:
            values(n - 1)
        softmax(n)
        if n + 1 < len(steps):
            logits(n + 1)
    values(len(steps) - 1)


def _fox(proj3, cols, tq=FOX_TQ, sub=FOX_SUB):
    batch, seq, _ = proj3.shape
    dh = FOX_HEAD_DIM
    tk = tq // 2
    nq = seq // tq

    def head(offset):
        return pl.BlockSpec((1, seq, dh), lambda b, h: (b, 0, offset + h))

    return pl.pallas_call(
        functools.partial(_fox_kernel, tq=tq, sub=sub, seq=seq),
        grid=(batch, FOX_HEADS),
        in_specs=[head(0), head(FOX_HEADS), head(2 * FOX_HEADS),
                  pl.BlockSpec((1, seq, V7X_LANES), lambda b, h: (b, 0, 0))],
        out_specs=head(0),
        out_shape=jax.ShapeDtypeStruct((batch, seq, FOX_WIDTH), BF16),
        scratch_shapes=[pltpu.VMEM((seq // tk, 2 * dh, tk), BF16), pltpu.VMEM((seq, 2 * dh), BF16),
                        pltpu.VMEM((seq, 2 * dh), BF16), pltpu.VMEM((nq, tq, V7X_LANES), F32),
                        pltpu.VMEM((nq, tq, 2 * dh), F32), pltpu.VMEM((2, tq, tk), F32),
                        pltpu.VMEM((2, tq, tk), BF16), pltpu.VMEM((2, tq, V7X_LANES), F32)],
        compiler_params=_params("parallel", "parallel"),
        name="fox_attn",
    )(proj3, proj3, proj3, cols)


MLSTM_GROUP = 2


def _mlstm_kernel(q_ref, k_ref, v_ref, o_ref, gn_ref, rows_ref, cols_ref, w0_ref, w1_ref, w2_ref,
                  w3_ref, y_ref, w0b_ref, w1b_ref, w2b_ref, w3b_ref, c_ref, n_ref, m_ref, *, batch):
    j = pl.program_id(0)
    L = CHUNK
    dh = MLSTM_HEAD_DIM

    @pl.when(j == 0)
    def _():
        c_ref[...] = jnp.zeros_like(c_ref)
        n_ref[...] = jnp.zeros_like(n_ref)
        m_ref[...] = jnp.zeros_like(m_ref)

    for src, dst in ((w0_ref, w0b_ref), (w1_ref, w1b_ref), (w2_ref, w2b_ref), (w3_ref, w3b_ref)):
        dst[...] = src[...].astype(BF16)

    lanes = V7X_LANES
    assert L == lanes and dh % lanes == 0
    halves = [slice(c0, c0 + lanes) for c0 in range(0, dh, lanes)]
    r = lax.broadcasted_iota(jnp.int32, (L, L), 0)
    c = lax.broadcasted_iota(jnp.int32, (L, L), 1)
    causal = c <= r
    all_chains = [(b, h) for b in range(batch) for h in range(MLSTM_HEADS)]

    def rep(x):
        return jnp.broadcast_to(x, (L, lanes))

    sel_r = lax.broadcasted_iota(jnp.int32, (3 * lanes, 2 * lanes), 0) % lanes
    sel_c = lax.broadcasted_iota(jnp.int32, (3 * lanes, 2 * lanes), 1)
    pieces = []
    for b in range(batch):
        hi, mid, lo = _split3(cols_ref[b])
        pieces.append(jnp.concatenate([hi, mid, lo], axis=1).astype(BF16))

    def run_group(chains):
        st1 = []
        for b, h in chains:
            hs = slice(h * dh, (h + 1) * dh)
            st = b * MLSTM_HEADS + h
            sel = sel_r == jnp.where(sel_c < lanes, ROW_MLB + h, ROW_MLI + h)
            bi = _dot(pieces[b], jnp.where(sel, 1.0, 0.0).astype(BF16))
            bcol, icol = bi[:, 0:lanes], bi[:, lanes:2 * lanes]
            qb = q_ref[b, :, hs]
            kb = k_ref[b, :, hs]
            brow = rows_ref[b, ROW_MLB + h:ROW_MLB + h + 1, :]
            irow = rows_ref[b, ROW_MLI + h:ROW_MLI + h + 1, :]
            m_prev = m_ref[st]
            c_prev = c_ref[st]
            qk_raw = _dot_nt(qb, kb)
            qc = _dot(qb, c_prev.astype(BF16))

            dlog = jnp.where(causal, bcol - brow + irow, -jnp.inf)
            inter = bcol + m_prev
            m_t = jnp.maximum(inter, rep(jnp.max(dlog, axis=-1, keepdims=True)))
            w = jnp.exp(dlog - m_t)
            a = jnp.exp(inter - m_t)
            b_end = brow[:, L - 1:L]
            m_new = jnp.maximum(b_end + m_prev, jnp.max(b_end - brow + irow, axis=-1, keepdims=True))
            decay = jnp.exp(b_end + m_prev - m_new)
            ws = jnp.exp(b_end - bcol + icol - m_new)
            qn = rep(jnp.sum(qb.astype(F32) * n_ref[st], axis=-1, keepdims=True))
            st1.append((hs, st, kb, c_prev, qk_raw, qc, m_t, w, a, m_new, decay, ws, qn))

        st2 = []
        for (b, h), (hs, st, kb, c_prev, qk_raw, qc, m_t, w, a, m_new, decay, ws, qn) in zip(chains, st1):
            v = v_ref[b, :, hs]
            qk = qk_raw * w
            k = kb.astype(F32)
            kw = jnp.concatenate([k[:, cs] * ws for cs in halves], axis=1)
            pv = _dot(qk.astype(BF16), v)
            upd = _dot_tn(kw.astype(BF16), v)
            st2.append((qk, kw, pv, upd))

        for (b, h), s1, (qk, kw, pv, upd) in zip(chains, st1, st2):
            hs, st, kb, c_prev, qk_raw, qc, m_t, w, a, m_new, decay, ws, qn = s1
            den = a * qn + rep(jnp.sum(qk, axis=-1, keepdims=True))
            inv = 1.0 / jnp.maximum(jnp.abs(den), jnp.exp(-m_t))
            h_parts = [(a * qc[:, cs] + pv[:, cs]) * inv for cs in halves]
            sq = h_parts[0] * h_parts[0]
            for hp in h_parts[1:]:
                sq = sq + hp * hp
            rms = lax.rsqrt(rep(jnp.sum(sq, axis=-1, keepdims=True)) * (1.0 / dh) + EPS)
            for cs, hp in zip(halves, h_parts):
                oc = slice(hs.start + cs.start, hs.start + cs.stop)
                y = hp * rms * gn_ref[:, oc] * o_ref[b, :, oc].astype(F32)
                y_ref[b, :, oc] = y.astype(BF16)
            c_ref[st] = decay * c_prev + upd
            n_ref[st] = decay * n_ref[st] + jnp.sum(kw, axis=0, keepdims=True)
            m_ref[st] = m_new

    for g0 in range(0, len(all_chains), MLSTM_GROUP):
        run_group(all_chains[g0:g0 + MLSTM_GROUP])


def _mlstm(proj3, gn, rows, cols, weights):
    batch, seq, _ = proj3.shape
    dh = MLSTM_HEAD_DIM
    width = MLSTM_WIDTH
    nc = seq // CHUNK
    chains = batch * MLSTM_HEADS

    def col(base):
        return lambda j: (0, j, base // width)

    shares = [w.shape[0] // nc for w in weights]
    assert all(s * nc == w.shape[0] and s % 16 == 0 for s, w in zip(shares, weights))
    w_specs = [pl.BlockSpec((s, w.shape[1]), lambda j: (j, 0)) for s, w in zip(shares, weights)]

    return pl.pallas_call(
        functools.partial(_mlstm_kernel, batch=batch),
        grid=(nc,),
        in_specs=[
            pl.BlockSpec((batch, CHUNK, width), col(COL_LQ)),
            pl.BlockSpec((batch, CHUNK, width), col(COL_LK)),
            pl.BlockSpec((batch, CHUNK, width), col(COL_LV)),
            pl.BlockSpec((batch, CHUNK, width), col(COL_LO)),
            pl.BlockSpec((1, width), lambda j: (0, 0)),
            pl.BlockSpec((batch, N_SMALL, CHUNK), lambda j: (0, 0, j)),
            pl.BlockSpec((batch, CHUNK, V7X_LANES), lambda j: (0, j, 0)),
        ] + w_specs,
        out_specs=[pl.BlockSpec((batch, CHUNK, width), lambda j: (0, j, 0))] + w_specs,
        out_shape=[jax.ShapeDtypeStruct((batch, seq, width), BF16)]
        + [jax.ShapeDtypeStruct(w.shape, BF16) for w in weights],
        scratch_shapes=[pltpu.VMEM((chains, dh, dh), F32), pltpu.VMEM((chains, 1, dh), F32),
                        pltpu.VMEM((chains, 1, 1), F32)],
        compiler_params=_params("arbitrary"),
        name="mlstm",
    )(proj3, proj3, proj3, proj3, gn, rows, cols, *weights)


def _merge_out_kernel(yf_ref, ym_ref, mq_ref, mk_ref, mv_ref, g0_ref, g1_ref, g2_ref, wf_ref, wm_ref,
                      wc_ref, wo_ref, x_ref, gn_ref, h_ref, u_ref):
    dh = MEM_HEAD_DIM
    heads = [slice(hh * dh, (hh + 1) * dh) for hh in range(MEM_HEADS)]
    logits = [_dot_nt(mq_ref[:, sl], mk_ref[0, :, sl]) for sl in heads]
    merged = g0_ref[...].astype(F32) * _dot(yf_ref[...], wf_ref[...])
    y_mem = []
    for sl, s in zip(heads, logits):
        p = jnp.exp(s - jnp.max(s, axis=-1, keepdims=True))
        l = jnp.sum(p, axis=-1, keepdims=True)
        y_mem.append((_dot(p.astype(BF16), mv_ref[0, :, sl]) / l).astype(BF16))
    merged = merged + g1_ref[...].astype(F32) * _dot(ym_ref[...], wm_ref[...])
    merged = merged + g2_ref[...].astype(F32) * _dot(jnp.concatenate(y_mem, axis=1), wc_ref[...])
    h = x_ref[...] + _dot(merged.astype(BF16), wo_ref[...])
    h_ref[...] = h
    u_ref[...] = _rms(h, gn_ref[...]).astype(BF16)


def _merge_out(y_fox, y_ml, proj, mkv3, w_f, w_m, w_c, w_o, x2, g_ffn, seq, bm=256):
    m = x2.shape[0]
    kdim = y_fox.shape[1]
    mem_len = mkv3.shape[1]
    assert seq % bm == 0
    tiles_per_seq = seq // bm
    y_spec = pl.BlockSpec((bm, kdim), lambda i: (i, 0))
    row_spec = pl.BlockSpec((bm, D_MODEL), lambda i: (i, 0))

    def gate_spec(branch):
        return pl.BlockSpec((bm, D_MODEL), lambda i: (i, COL_GATES // D_MODEL + branch))

    def resident(rows):
        return pl.BlockSpec((rows, D_MODEL), lambda i: (0, 0), pipeline_mode=pl.Buffered(1))

    def mem_spec(half):
        return pl.BlockSpec((1, mem_len, MEM_WIDTH), lambda i: (i // tiles_per_seq, 0, half))

    return pl.pallas_call(
        _merge_out_kernel,
        grid=(m // bm,),
        in_specs=[y_spec, y_spec,
                  pl.BlockSpec((bm, MEM_WIDTH), lambda i: (i, COL_MQ // MEM_WIDTH)),
                  mem_spec(0), mem_spec(1),
                  gate_spec(0), gate_spec(1), gate_spec(2),
                  resident(kdim), resident(kdim), resident(kdim), resident(D_MODEL),
                  row_spec, pl.BlockSpec((1, D_MODEL), lambda i: (0, 0))],
        out_specs=[row_spec, row_spec],
        out_shape=[
            jax.ShapeDtypeStruct((m, D_MODEL), F32),
            jax.ShapeDtypeStruct((m, D_MODEL), BF16),
        ],
        compiler_params=_params("parallel"),
        name="merge_out",
    )(y_fox, y_ml, proj, mkv3, mkv3, proj, proj, proj, w_f, w_m, w_c, w_o, x2, g_ffn)


def _ffn_in_kernel(u_ref, wg_ref, wu_ref, wo_ref, o_ref, wob_ref, wgb_ref, wub_ref):
    @pl.when(pl.program_id(1) == 0)
    def _():
        _cast_weight(wgb_ref, lambda sl: wg_ref[sl, :], D_MODEL)
        _cast_weight(wub_ref, lambda sl: wu_ref[sl, :], D_MODEL)

    wob_ref[...] = wo_ref[...].astype(BF16)
    for r0 in range(0, u_ref.shape[0], EPILOGUE_ROWS):
        rs = slice(r0, r0 + EPILOGUE_ROWS)
        u = u_ref[rs, :]
        gate = _dot(u, wgb_ref[...])
        up = _dot(u, wub_ref[...])
        o_ref[rs, :] = (gate * jax.nn.sigmoid(gate) * up).astype(BF16)


def _ffn_in(u, w, w_out_f32, bm=1024, bn=512):
    m = u.shape[0]
    nb = D_FF // bn
    mt = m // bm
    share = D_FF // (nb * mt)
    assert share * nb * mt == D_FF and share % 16 == 0
    return pl.pallas_call(
        _ffn_in_kernel,
        grid=(nb, mt),
        in_specs=[
            pl.BlockSpec((bm, D_MODEL), lambda j, i: (i, 0)),
            pl.BlockSpec((D_MODEL, bn), lambda j, i: (0, j)),
            pl.BlockSpec((D_MODEL, bn), lambda j, i: (0, nb + j)),
            pl.BlockSpec((share, D_MODEL), lambda j, i: (j * mt + i, 0)),
        ],
        out_specs=[
            pl.BlockSpec((bm, bn), lambda j, i: (i, j)),
            pl.BlockSpec((share, D_MODEL), lambda j, i: (j * mt + i, 0)),
        ],
        out_shape=[
            jax.ShapeDtypeStruct((m, D_FF), BF16),
            jax.ShapeDtypeStruct((D_FF, D_MODEL), BF16),
        ],
        scratch_shapes=[pltpu.VMEM((D_MODEL, bn), BF16)] * 2,
        compiler_params=_params("arbitrary", "arbitrary"),
        name="ffn_in",
    )(u, w, w, w_out_f32)


def _ffn_out_kernel(a_ref, w_ref, r_ref, g_ref, o_ref, *, final_norm):
    h = r_ref[...] + _dot(a_ref[...], w_ref[...])
    o_ref[...] = _rms(h, g_ref[...]) if final_norm else h


def _ffn_out(act, w_bf16, resid, final_gain, bm=256):
    m = act.shape[0]
    final_norm = final_gain is not None
    gain = final_gain if final_norm else jnp.ones((1, D_MODEL), F32)
    return pl.pallas_call(
        functools.partial(_ffn_out_kernel, final_norm=final_norm),
        grid=(m // bm,),
        in_specs=[
            pl.BlockSpec((bm, D_FF), lambda i: (i, 0)),
            pl.BlockSpec((D_FF, D_MODEL), lambda i: (0, 0), pipeline_mode=pl.Buffered(1)),
            pl.BlockSpec((bm, D_MODEL), lambda i: (i, 0)),
            pl.BlockSpec((1, D_MODEL), lambda i: (0, 0)),
        ],
        out_specs=pl.BlockSpec((bm, D_MODEL), lambda i: (i, 0)),
        out_shape=jax.ShapeDtypeStruct((m, D_MODEL), F32),
        compiler_params=_params("parallel"),
        name="ffn_out",
    )(act, w_bf16, resid, gain)


def _layer(h2, mem2, batch, seq, norm_mix, w_in, b_in, conv_w, conv_b, mlstm_norm, norm_mem,
           w_mem_kv, w_br_fox, w_br_mlstm, w_br_mem, w_out, norm_ffn, w_ffn_in, w_ffn_out,
           final_gain):
    b_big = jnp.concatenate([b_in[:OFF_FF], b_in[OFF_LQ:OFF_LI], b_in[OFF_LO:]])[None, :]
    b_small_t = jnp.concatenate([b_in[OFF_FF:OFF_LQ], b_in[OFF_LI:OFF_LO]])[:, None]

    w_in_t = w_in.T
    u, gst, mkv = _norm_in(h2, norm_mix[None, :], w_in_t, b_small_t, mem2, norm_mem[None, :], w_mem_kv)
    proj = _in_proj(u, w_in_t, b_big, conv_w, conv_b[None, :], seq)
    rows, cols = _gates(gst, batch, seq)
    proj3 = proj.reshape(batch, seq, PROJ_BIG)

    y_fox = _fox(proj3, cols)
    y_ml, wf_b, wm_b, wc_b, wo_b = _mlstm(proj3, mlstm_norm[None, :], rows, cols,
                                          (w_br_fox, w_br_mlstm, w_br_mem, w_out))
    tokens = batch * seq
    h2, u_ffn = _merge_out(y_fox.reshape(tokens, FOX_WIDTH), y_ml.reshape(tokens, MLSTM_WIDTH),
                           proj, mkv.reshape(batch, -1, 2 * MEM_WIDTH), wf_b, wm_b, wc_b, wo_b,
                           h2, norm_ffn[None, :], seq)
    act, w_ffn_out_bf16 = _ffn_in(u_ffn, w_ffn_in, w_ffn_out)
    return _ffn_out(act, w_ffn_out_bf16, h2, final_gain)


def kernel(x, mem, norm_mix, w_in, b_in, conv_w, conv_b, mlstm_norm, norm_mem, w_mem_kv, w_br_fox,
           w_br_mlstm, w_br_mem, w_out, norm_ffn, w_ffn_in, w_ffn_out, norm_final):
    batch, seq, d = x.shape
    assert d == D_MODEL and seq % CHUNK == 0
    h2 = x.reshape(batch * seq, d)
    mem2 = mem.reshape(batch * mem.shape[1], d)
    depth = norm_mix.shape[0]
    for l in range(depth):
        final_gain = norm_final[None, :] if l == depth - 1 else None
        h2 = _layer(h2, mem2, batch, seq, norm_mix[l], w_in[l], b_in[l], conv_w[l], conv_b[l],
                    mlstm_norm[l], norm_mem[l], w_mem_kv[l], w_br_fox[l], w_br_mlstm[l],
                    w_br_mem[l], w_out[l], norm_ffn[l], w_ffn_in[l], w_ffn_out[l], final_gain)
    return h2.reshape(batch, seq, d)
```
